```python
import jax, jax.numpy as jnp
from jax import lax
import numpy as np

D_MODEL = 1024
BATCH = 16
SEQ = 4096
DEPTH = 1
DEC_BATCH = 8
DEC_SEQ = 64
PAST_LEN = 4096

CHUNK = 64
RET_HEADS = 4
RET_QK_DIM = 256
RET_V_DIM = 512
RET_QK = RET_HEADS * RET_QK_DIM
RET_V = RET_HEADS * RET_V_DIM
CONV_K = 3
CONV_CH = D_MODEL
N_EXPERTS = 32
TOP_K = 4
D_FF = D_MODEL
SWIGLU_LIMIT = 7.0
SWIGLU_ALPHA = 1.702
ROPE_BASE = 10000.0
EPS = 1e-6
PROJ_SIZES = (RET_QK, RET_QK, RET_V, RET_V, CONV_CH, CONV_CH, CONV_CH, D_MODEL, D_MODEL)
PROJ_WIDTH = sum(PROJ_SIZES)

kernel_name = 'hybrid_conv_retention_moe_stream'


def _split_points():
    return [int(i) for i in np.cumsum(PROJ_SIZES)[:-1]]


def _rmsnorm(x, g):
    x32 = x.astype(jnp.float32)
    y = x32 * lax.rsqrt(jnp.mean(x32 * x32, axis=-1, keepdims=True) + EPS)
    return (y * g.astype(jnp.float32)).astype(x.dtype)


def _rope(x, pos):
    d = x.shape[-1]
    inv = ROPE_BASE ** (-jnp.arange(0, d, 2, dtype=jnp.float32) / d)
    ang = pos[:, None] * inv[None, :]
    cos = jnp.cos(ang)[None, :, None, :]
    sin = jnp.sin(ang)[None, :, None, :]
    x32 = x.astype(jnp.float32)
    x1, x2 = x32[..., : d // 2], x32[..., d // 2:]
    return jnp.concatenate([x1 * cos - x2 * sin, x1 * sin + x2 * cos], axis=-1)


def _log_gamma():
    return jnp.log(1.0 - jnp.exp2(-5.0 - jnp.arange(RET_HEADS, dtype=jnp.float32)))


def _retention_block(S, q, k, v, lg):
    L = q.shape[2]
    idx = jnp.arange(L, dtype=jnp.float32)
    dist = jnp.abs(idx[:, None] - idx[None, :])
    decay = jnp.exp(dist[None] * lg[:, None, None])
    scores = jnp.einsum('bhid,bhjd->bhij', q, k) * decay
    o = jnp.einsum('bhij,bhjv->bhiv', scores, v)
    q_dec = jnp.exp((idx + 1.0)[None, :] * lg[:, None])
    o = o + jnp.einsum('bhid,bhdv->bhiv', q, S) * q_dec[None, :, :, None]
    k_dec = jnp.exp((L - 1.0 - idx)[None, :] * lg[:, None])
    S_new = jnp.exp(L * lg)[None, :, None, None] * S + jnp.einsum(
        'bhjd,bhjv->bhdv', k * k_dec[None, :, :, None], v)
    return o, S_new


def _retention(q, k, v, S0):
    B, T = q.shape[0], q.shape[1]
    L = CHUNK if T % CHUNK == 0 else T
    n = T // L
    lg = _log_gamma()

    def to_chunks(a):
        return a.reshape(B, n, L, RET_HEADS, a.shape[-1]).transpose(1, 0, 3, 2, 4)

    def step(S, qkv):
        o, S = _retention_block(S, qkv[0], qkv[1], qkv[2], lg)
        return S, o

    S_fin, o = lax.scan(step, S0, (to_chunks(q), to_chunks(k), to_chunks(v)))
    o = o.transpose(1, 0, 3, 2, 4).reshape(B, T, RET_HEADS, RET_V_DIM)
    return o, S_fin


def _mixer(xn, pos0, conv_cache, S0, w_in, conv_w, w_ret_out, w_conv_out, w_mix_out):
    B, T, _ = xn.shape
    proj = xn @ w_in
    q, k, v, g, cb, cc, cx, ga, gb = jnp.split(proj, _split_points(), axis=-1)
    u = cc * cx
    u_pad = jnp.concatenate([conv_cache.astype(u.dtype), u], axis=1)
    conv = u_pad[:, 0:T] * conv_w[0]
    for j in range(1, CONV_K):
        conv = conv + u_pad[:, j:j + T] * conv_w[j]
    y_conv = (cb * conv) @ w_conv_out
    new_conv = u_pad[:, -(CONV_K - 1):]
    pos = jnp.arange(T, dtype=jnp.float32) + float(pos0)
    qh = _rope(q.reshape(B, T, RET_HEADS, RET_QK_DIM), pos)
    kh = _rope(k.reshape(B, T, RET_HEADS, RET_QK_DIM), pos) * (RET_QK_DIM ** -0.5)
    vh = v.reshape(B, T, RET_HEADS, RET_V_DIM).astype(jnp.float32)
    o, S_new = _retention(qh, kh, vh, S0.astype(jnp.float32))
    o = o * lax.rsqrt(jnp.mean(o * o, axis=-1, keepdims=True) + EPS)
    o = o.reshape(B, T, RET_V).astype(xn.dtype)
    y_ret = (jax.nn.silu(g) * o) @ w_ret_out
    h = jax.nn.sigmoid(ga) * y_conv + jax.nn.sigmoid(gb) * y_ret
    return h @ w_mix_out, new_conv, S_new


def _moe(xn, w_router, b_router, w_gate, b_gate, w_up, b_up, w_down, b_down):
    B, T, D = xn.shape
    xt = xn.reshape(B * T, D)
    logits = (xt @ w_router + b_router).astype(jnp.float32)
    top_v, top_i = lax.top_k(logits, TOP_K)
    probs = jax.nn.softmax(top_v, axis=-1)
    combine = jnp.einsum('nk,nke->ne', probs,
                         jax.nn.one_hot(top_i, N_EXPERTS, dtype=jnp.float32)).astype(xt.dtype)
    y = jnp.zeros_like(xt)
    for e in range(N_EXPERTS):
        gl = jnp.minimum(xt @ w_gate[e] + b_gate[e], SWIGLU_LIMIT)
        lin = jnp.clip(xt @ w_up[e] + b_up[e], -SWIGLU_LIMIT, SWIGLU_LIMIT)
        act = gl * jax.nn.sigmoid(SWIGLU_ALPHA * gl) * (lin + 1.0)
        y = y + combine[:, e:e + 1] * (act @ w_down[e] + b_down[e])
    return y.reshape(B, T, D)


def _trunk(x, c, pos0, conv_caches, ret_states, w_ada, b_ada, g_norm1, w_in, conv_w,
           w_ret_out, w_conv_out, w_mix_out, g_norm2, w_router, b_router,
           w_gate, b_gate, w_up, b_up, w_down, b_down, g_final):
    new_conv, new_ret = [], []
    for l in range(DEPTH):
        mod = jax.nn.silu(c) @ w_ada[l] + b_ada[l]
        sh1, sc1, gt1, sh2, sc2, gt2 = [m[:, None, :] for m in jnp.split(mod, 6, axis=-1)]
        xn = _rmsnorm(x, g_norm1[l]) * (1.0 + sc1) + sh1
        mix, cst, sst = _mixer(xn, pos0, conv_caches[l], ret_states[l], w_in[l], conv_w[l],
                               w_ret_out[l], w_conv_out[l], w_mix_out[l])
        x = x + gt1 * mix
        xn = _rmsnorm(x, g_norm2[l]) * (1.0 + sc2) + sh2
        x = x + gt2 * _moe(xn, w_router[l], b_router[l], w_gate[l], b_gate[l],
                           w_up[l], b_up[l], w_down[l], b_down[l])
        new_conv.append(cst)
        new_ret.append(sst)
    return _rmsnorm(x, g_final), jnp.stack(new_conv), jnp.stack(new_ret)


def setup_inputs(seed: int = 0) -> dict:
    key = jax.random.key(seed)
    ks = jax.random.split(key, 32)
    f32 = jnp.float32
    D = D_MODEL

    def nrm(k, shape, scale):
        return jax.random.normal(k, shape, f32) * scale

    return {
        'x_prompt': nrm(ks[0], (BATCH, SEQ, D), 1.0),
        'x_sample': nrm(ks[1], (DEC_BATCH, DEC_SEQ, D), 1.0),
        'c_prompt': nrm(ks[2], (BATCH, D), 1.0),
        'c_sample': nrm(ks[3], (DEC_BATCH, D), 1.0),
        'cache_conv': nrm(ks[4], (DEPTH, DEC_BATCH, CONV_K - 1, CONV_CH), 1.0),
        'state_ret': nrm(ks[5], (DEPTH, DEC_BATCH, RET_HEADS, RET_QK_DIM, RET_V_DIM), 0.5),
        'w_ada': nrm(ks[6], (DEPTH, D, 6 * D), 0.5 * D ** -0.5),
        'b_ada': nrm(ks[7], (DEPTH, 6 * D), 0.02),
        'g_norm1': 1.0 + nrm(ks[8], (DEPTH, D), 0.01),
        'w_in': nrm(ks[9], (DEPTH, D, PROJ_WIDTH), D ** -0.5),
        'conv_w': nrm(ks[10], (DEPTH, CONV_K, CONV_CH), CONV_K ** -0.5),
        'w_ret_out': nrm(ks[11], (DEPTH, RET_V, D), RET_V ** -0.5),
        'w_conv_out': nrm(ks[12], (DEPTH, CONV_CH, D), CONV_CH ** -0.5),
        'w_mix_out': nrm(ks[13], (DEPTH, D, D), D ** -0.5),
        'g_norm2': 1.0 + nrm(ks[14], (DEPTH, D), 0.01),
        'w_router': nrm(ks[15], (DEPTH, D, N_EXPERTS), D ** -0.5),
        'b_router': nrm(ks[16], (DEPTH, N_EXPERTS), 0.01),
        'w_gate': nrm(ks[17], (DEPTH, N_EXPERTS, D, D_FF), D ** -0.5),
        'b_gate': nrm(ks[18], (DEPTH, N_EXPERTS, D_FF), 0.01),
        'w_up': nrm(ks[19], (DEPTH, N_EXPERTS, D, D_FF), D ** -0.5),
        'b_up': nrm(ks[20], (DEPTH, N_EXPERTS, D_FF), 0.01),
        'w_down': nrm(ks[21], (DEPTH, N_EXPERTS, D_FF, D), D_FF ** -0.5),
        'b_down': nrm(ks[22], (DEPTH, N_EXPERTS, D), 0.01),
        'g_final': 1.0 + nrm(ks[23], (D,), 0.01),
    }


def reference(x_prompt, x_sample, c_prompt, c_sample, cache_conv, state_ret,
              w_ada, b_ada, g_norm1, w_in, conv_w, w_ret_out, w_conv_out, w_mix_out,
              g_norm2, w_router, b_router, w_gate, b_gate, w_up, b_up, w_down, b_down,
              g_final):
    Bp = x_prompt.shape[0]
    conv0 = jnp.zeros((DEPTH, Bp, CONV_K - 1, CONV_CH), x_prompt.dtype)
    ret0 = jnp.zeros((DEPTH, Bp, RET_HEADS, RET_QK_DIM, RET_V_DIM), jnp.float32)
    y_prompt, conv_p, ret_p = _trunk(
        x_prompt, c_prompt, 0, conv0, ret0, w_ada, b_ada, g_norm1, w_in, conv_w,
        w_ret_out, w_conv_out, w_mix_out, g_norm2, w_router, b_router,
        w_gate, b_gate, w_up, b_up, w_down, b_down, g_final)
    y_sample, conv_s, ret_s = _trunk(
        x_sample, c_sample, PAST_LEN, cache_conv, state_ret, w_ada, b_ada, g_norm1, w_in,
        conv_w, w_ret_out, w_conv_out, w_mix_out, g_norm2, w_router, b_router,
        w_gate, b_gate, w_up, b_up, w_down, b_down, g_final)
    return (y_prompt, y_sample, conv_p, ret_p, conv_s, ret_s)
```

```python
import functools

import jax
import jax.numpy as jnp
import numpy as np
from jax import lax
from jax.experimental import pallas as pl
from jax.experimental.pallas import tpu as pltpu

D_MODEL = 1024
CHUNK = 64
RET_HEADS = 4
RET_QK_DIM = 256
RET_V_DIM = 512
RET_QK = RET_HEADS * RET_QK_DIM
RET_V = RET_HEADS * RET_V_DIM
CONV_K = 3
N_EXPERTS = 32
TOP_K = 4
SWIGLU_LIMIT = 7.0
SWIGLU_ALPHA = 1.702
ROPE_BASE = 10000.0
PAST_LEN = 4096
EPS = 1e-6
PROJ_WIDTH = 2 * RET_QK + 2 * RET_V + 5 * D_MODEL

V7X_VMEM_LIMIT_BYTES = 56 * 1024 * 1024
RET_BLOCK = 256
PROJ_ROWS = 256
MOE_ROWS = 1024

F32 = jnp.float32
BF16 = jnp.bfloat16


def _params(*sem):
    return pltpu.CompilerParams(dimension_semantics=sem, vmem_limit_bytes=V7X_VMEM_LIMIT_BYTES)


def _sigmoid(x):
    return 1.0 / (1.0 + jnp.exp(-x))


def _rms(x):
    return x * lax.rsqrt(jnp.mean(x * x, axis=-1, keepdims=True) + EPS)


def _ada_kernel(c_ref, w_ref, b_ref, o_ref):
    c = c_ref[...]
    s = c * _sigmoid(c)
    o_ref[...] = jnp.dot(s, w_ref[...], preferred_element_type=F32,
                         precision=lax.Precision.HIGHEST) + b_ref[...]


def _ada(c, w, b):
    n, d = c.shape
    width = w.shape[1]
    tn = 1024
    return pl.pallas_call(
        _ada_kernel,
        grid=(width // tn,),
        in_specs=[pl.BlockSpec((n, d), lambda j: (0, 0)),
                  pl.BlockSpec((d, tn), lambda j: (0, j)),
                  pl.BlockSpec((1, tn), lambda j: (0, j))],
        out_specs=pl.BlockSpec((n, tn), lambda j: (0, j)),
        out_shape=jax.ShapeDtypeStruct((n, width), F32),
        compiler_params=_params("arbitrary"),
        name="ada",
    )(c, w, b.reshape(1, width))


def _in_proj_kernel(x_ref, mod_ref, g1_ref, cos_ref, sin_ref, w_ref,
                    q_ref, k_ref, v_ref, gs_ref, cb_ref, u_ref, sga_ref, sgb_ref):
    x = x_ref[0]
    mod = mod_ref[0]
    xn = _rms(x) * g1_ref[...]
    xn = (xn * (1.0 + mod[1:2]) + mod[0:1]).astype(BF16)
    cos = cos_ref[...]
    sin = sin_ref[...]

    def proj(c0, width):
        return jnp.dot(xn, w_ref[:, c0:c0 + width], preferred_element_type=F32)

    def rope_store(acc, out_ref, scale):
        half = RET_QK_DIM // 2
        for h in range(RET_HEADS):
            a = acc[:, h * RET_QK_DIM: h * RET_QK_DIM + half]
            b = acc[:, h * RET_QK_DIM + half: (h + 1) * RET_QK_DIM]
            out_ref[0, :, h * RET_QK_DIM: h * RET_QK_DIM + half] = (
                (a * cos - b * sin) * scale).astype(BF16)
            out_ref[0, :, h * RET_QK_DIM + half: (h + 1) * RET_QK_DIM] = (
                (a * sin + b * cos) * scale).astype(BF16)

    c0 = 0
    rope_store(proj(c0, RET_QK), q_ref, 1.0)
    c0 += RET_QK
    rope_store(proj(c0, RET_QK), k_ref, RET_QK_DIM ** -0.5)
    c0 += RET_QK
    for j in range(RET_V // 1024):
        v_ref[0, :, j * 1024:(j + 1) * 1024] = proj(c0 + j * 1024, 1024).astype(BF16)
    c0 += RET_V
    for j in range(RET_V // 1024):
        g = proj(c0 + j * 1024, 1024)
        gs_ref[0, :, j * 1024:(j + 1) * 1024] = (g * _sigmoid(g)).astype(BF16)
    c0 += RET_V
    cb_ref[0] = proj(c0, D_MODEL).astype(BF16)
    c0 += D_MODEL
    cc = proj(c0, D_MODEL)
    c0 += D_MODEL
    u_ref[0] = cc * proj(c0, D_MODEL)
    c0 += D_MODEL
    sga_ref[0] = _sigmoid(proj(c0, D_MODEL)).astype(BF16)
    c0 += D_MODEL
    sgb_ref[0] = _sigmoid(proj(c0, D_MODEL)).astype(BF16)


def _in_proj(x, mod, g1, cos, sin, w_in_bf16):
    bsz, t, d = x.shape
    tm = min(PROJ_ROWS, t)
    row = lambda width: pl.BlockSpec((1, tm, width), lambda b, i: (b, i, 0))
    shp = lambda width, dt: jax.ShapeDtypeStruct((bsz, t, width), dt)
    return pl.pallas_call(
        _in_proj_kernel,
        grid=(bsz, t // tm),
        in_specs=[row(d),
                  pl.BlockSpec((1, 6, d), lambda b, i: (b, 0, 0)),
                  pl.BlockSpec((1, d), lambda b, i: (0, 0)),
                  pl.BlockSpec((tm, RET_QK_DIM // 2), lambda b, i: (i, 0)),
                  pl.BlockSpec((tm, RET_QK_DIM // 2), lambda b, i: (i, 0)),
                  pl.BlockSpec((d, PROJ_WIDTH), lambda b, i: (0, 0),
                               pipeline_mode=pl.Buffered(1))],
        out_specs=[row(RET_QK), row(RET_QK), row(RET_V), row(RET_V),
                   row(d), row(d), row(d), row(d)],
        out_shape=[shp(RET_QK, BF16), shp(RET_QK, BF16), shp(RET_V, BF16), shp(RET_V, BF16),
                   shp(d, BF16), shp(d, F32), shp(d, BF16), shp(d, BF16)],
        compiler_params=_params("parallel", "parallel"),
        name="in_proj",
    )(x, mod, g1, cos, sin, w_in_bf16)


def _conv_kernel(u_ref, cb_ref, cache_ref, cw_ref, w_ref, y_ref, new_ref, prev_ref):
    i = pl.program_id(1)

    @pl.when(i == 0)
    def _():
        prev_ref[...] = cache_ref[0]

    u = u_ref[0]
    tc = u.shape[0]
    prev = prev_ref[...]
    rows = lax.broadcasted_iota(jnp.int32, u.shape, 0)
    u1 = jnp.where(rows == 0, prev[1:2], pltpu.roll(u, 1, axis=0))
    u2 = jnp.where(rows == 0, prev[0:1], jnp.where(rows == 1, prev[1:2], pltpu.roll(u, 2, axis=0)))
    cw = cw_ref[...]
    conv = u2 * cw[0:1] + u1 * cw[1:2] + u * cw[2:3]
    z = (cb_ref[0].astype(F32) * conv).astype(BF16)
    y_ref[0] = jnp.dot(z, w_ref[...], preferred_element_type=F32).astype(BF16)
    tail = u_ref[0, tc - 2:tc, :]
    prev_ref[...] = tail
    new_ref[0] = tail


def _conv(u, cb, cache, conv_w, w_conv_out_bf16):
    bsz, t, d = u.shape
    tc = min(512, t)
    return pl.pallas_call(
        _conv_kernel,
        grid=(bsz, t // tc),
        in_specs=[pl.BlockSpec((1, tc, d), lambda b, i: (b, i, 0)),
                  pl.BlockSpec((1, tc, d), lambda b, i: (b, i, 0)),
                  pl.BlockSpec((1, CONV_K - 1, d), lambda b, i: (b, 0, 0)),
                  pl.BlockSpec((CONV_K, d), lambda b, i: (0, 0)),
                  pl.BlockSpec((d, d), lambda b, i: (0, 0))],
        out_specs=[pl.BlockSpec((1, tc, d), lambda b, i: (b, i, 0)),
                   pl.BlockSpec((1, CONV_K - 1, d), lambda b, i: (b, 0, 0))],
        out_shape=[jax.ShapeDtypeStruct((bsz, t, d), BF16),
                   jax.ShapeDtypeStruct((bsz, CONV_K - 1, d), F32)],
        scratch_shapes=[pltpu.VMEM((CONV_K - 1, d), F32)],
        compiler_params=_params("parallel", "arbitrary"),
        name="conv",
    )(u, cb, cache, conv_w, w_conv_out_bf16)


def _decay_tables(block):
    lg = np.log(1.0 - np.exp2(-5.0 - np.arange(RET_HEADS, dtype=np.float64)))
    idx = np.arange(block, dtype=np.float64)
    dist = np.abs(idx[:, None] - idx[None, :])
    visible = (idx[None, :] // CHUNK) <= (idx[:, None] // CHUNK)
    dmat = np.where(visible[None], np.exp(dist[None] * lg[:, None, None]), 0.0)
    qdec = np.exp((idx + 1.0)[None, :] * lg[:, None])
    kdec = np.exp((block - 1.0 - idx)[None, :] * lg[:, None])
    sdec = np.exp(block * lg)
    lanes = lambda a: jnp.asarray(np.broadcast_to(a[:, :, None], (RET_HEADS, block, 128)), F32)
    return jnp.asarray(dmat, F32), lanes(qdec), lanes(kdec), [float(s) for s in sdec]


def _ret_kernel(sdec, q_ref, k_ref, v_ref, gs_ref, yc_ref, sga_ref, sgb_ref, x_ref, mod_ref,
                s0_ref, dmat_ref, qdec_ref, kdec_ref, wro_ref, wmo_ref, g2_ref,
                x1_ref, xn2_ref, s_ref):
    i = pl.program_id(1)

    @pl.when(i == 0)
    def _():
        s_ref[...] = s0_ref[...]

    ls = q_ref.shape[1]
    yret = jnp.zeros((ls, D_MODEL), F32)
    for h in range(RET_HEADS):
        qh = q_ref[0, :, h * RET_QK_DIM:(h + 1) * RET_QK_DIM]
        kh = k_ref[0, :, h * RET_QK_DIM:(h + 1) * RET_QK_DIM]
        vh = v_ref[0, :, h * RET_V_DIM:(h + 1) * RET_V_DIM]
        state = s_ref[0, h]
        scores = lax.dot_general(qh, kh, (((1,), (1,)), ((), ())), preferred_element_type=F32)
        p = (scores * dmat_ref[h]).astype(BF16)
        o = jnp.dot(p, vh, preferred_element_type=F32)
        qdec = jnp.concatenate([qdec_ref[h]] * (RET_V_DIM // 128), axis=1)
        o = o + jnp.dot(qh, state.astype(BF16), preferred_element_type=F32) * qdec
        kdec = jnp.concatenate([kdec_ref[h]] * (RET_QK_DIM // 128), axis=1)
        kd = (kh.astype(F32) * kdec).astype(BF16)
        s_ref[0, h] = sdec[h] * state + lax.dot_general(
            kd, vh, (((0,), (0,)), ((), ())), preferred_element_type=F32)
        og = (gs_ref[0, :, h * RET_V_DIM:(h + 1) * RET_V_DIM].astype(F32) * _rms(o)).astype(BF16)
        yret = yret + jnp.dot(og, wro_ref[h * RET_V_DIM:(h + 1) * RET_V_DIM, :],
                              preferred_element_type=F32)
    hmix = (sga_ref[0].astype(F32) * yc_ref[0].astype(F32) + sgb_ref[0].astype(F32) * yret)
    mix = jnp.dot(hmix.astype(BF16), wmo_ref[...], preferred_element_type=F32)
    mod = mod_ref[0]
    x1 = x_ref[0] + mod[2:3] * mix
    x1_ref[0] = x1
    xn2 = _rms(x1) * g2_ref[...]
    xn2_ref[0] = (xn2 * (1.0 + mod[4:5]) + mod[3:4]).astype(BF16)


def _ret(q, k, v, gs, yc, sga, sgb, x, mod, s0, w_ret_out_bf16, w_mix_out_bf16, g2):
    bsz, t, d = x.shape
    ls = min(RET_BLOCK, t)
    dmat, qdec, kdec, sdec = _decay_tables(ls)
    row = lambda width: pl.BlockSpec((1, ls, width), lambda b, i: (b, i, 0))
    const = lambda shape: pl.BlockSpec(shape, lambda b, i: (0,) * len(shape))
    state_spec = pl.BlockSpec((1, RET_HEADS, RET_QK_DIM, RET_V_DIM), lambda b, i: (b, 0, 0, 0))
    return pl.pallas_call(
        functools.partial(_ret_kernel, sdec),
        grid=(bsz, t // ls),
        in_specs=[row(RET_QK), row(RET_QK), row(RET_V), row(RET_V), row(d), row(d), row(d), row(d),
                  pl.BlockSpec((1, 6, d), lambda b, i: (b, 0, 0)),
                  state_spec,
                  const((RET_HEADS, ls, ls)), const((RET_HEADS, ls, 128)),
                  const((RET_HEADS, ls, 128)),
                  const((RET_V, d)), const((d, d)), const((1, d))],
        out_specs=[row(d), row(d), state_spec],
        out_shape=[jax.ShapeDtypeStruct((bsz, t, d), F32),
                   jax.ShapeDtypeStruct((bsz, t, d), BF16),
                   jax.ShapeDtypeStruct(s0.shape, F32)],
        compiler_params=_params("parallel", "arbitrary"),
        name="ret",
    )(q, k, v, gs, yc, sga, sgb, x, mod, s0, dmat, qdec, kdec, w_ret_out_bf16, w_mix_out_bf16, g2)


def _router_kernel(xn_ref, w_ref, b_ref, comb_ref):
    logits = jnp.dot(xn_ref[...], w_ref[...], preferred_element_type=F32) + b_ref[...]
    lane = lax.broadcasted_iota(jnp.int32, logits.shape, 1)
    work = logits
    top = None
    comb = jnp.zeros(logits.shape, F32)
    for _ in range(TOP_K):
        m = jnp.max(work, axis=-1, keepdims=True)
        first = jnp.min(jnp.where(work == m, lane, N_EXPERTS), axis=-1, keepdims=True)
        sel = lane == first
        if top is None:
            top = m
        comb = comb + jnp.where(sel, jnp.exp(m - top), 0.0)
        work = jnp.where(sel, -jnp.inf, work)
    comb_ref[...] = comb / jnp.sum(comb, axis=-1, keepdims=True)


def _router(xn2, w_router_bf16, b_router):
    n, d = xn2.shape
    tm = min(1024, n)
    return pl.pallas_call(
        _router_kernel,
        grid=(n // tm,),
        in_specs=[pl.BlockSpec((tm, d), lambda i: (i, 0)),
                  pl.BlockSpec((d, N_EXPERTS), lambda i: (0, 0)),
                  pl.BlockSpec((1, N_EXPERTS), lambda i: (0, 0))],
        out_specs=pl.BlockSpec((tm, N_EXPERTS), lambda i: (i, 0)),
        out_shape=jax.ShapeDtypeStruct((n, N_EXPERTS), F32),
        compiler_params=_params("parallel"),
        name="router",
    )(xn2, w_router_bf16, b_router.reshape(1, N_EXPERTS))


def _moe_kernel(t, xn_ref, comb_ref, x1_ref, mod_ref, wg_ref, bg_ref, wu_ref, bu_ref, wd_ref, bd_ref,
                gf_ref, y_ref, acc_ref):
    e = pl.program_id(1)

    @pl.when(e == 0)
    def _():
        acc_ref[...] = jnp.zeros_like(acc_ref)

    xn = xn_ref[...]
    gl = jnp.minimum(jnp.dot(xn, wg_ref[0], preferred_element_type=F32) + bg_ref[0], SWIGLU_LIMIT)
    lin = jnp.clip(jnp.dot(xn, wu_ref[0], preferred_element_type=F32) + bu_ref[0],
                   -SWIGLU_LIMIT, SWIGLU_LIMIT)
    act = gl * _sigmoid(SWIGLU_ALPHA * gl) * (lin + 1.0)
    out = jnp.dot(act.astype(BF16), wd_ref[0], preferred_element_type=F32) + bd_ref[0]
    comb = comb_ref[...]
    lane = lax.broadcasted_iota(jnp.int32, comb.shape, 1)
    weight = jnp.sum(jnp.where(lane == e, comb, 0.0), axis=-1, keepdims=True)
    acc_ref[...] += weight * out

    @pl.when(e == N_EXPERTS - 1)
    def _():
        tm = acc_ref.shape[0]
        i = pl.program_id(0)
        seg = min(t, tm)
        for j in range(tm // seg):
            b = (i * tm + j * seg) // t
            rows = slice(j * seg, (j + 1) * seg)
            x2 = x1_ref[rows, :] + mod_ref[b, 5:6, :] * acc_ref[rows, :]
            y_ref[rows, :] = _rms(x2) * gf_ref[...]


def _moe(xn2, comb, x1, mod, t, wg, bg, wu, bu, wd, bd, g_final):
    n, d = xn2.shape
    tm = min(MOE_ROWS, n)
    assert tm % t == 0 or t % tm == 0
    tok = lambda width: pl.BlockSpec((tm, width), lambda i, e: (i, 0))
    wspec = pl.BlockSpec((1, d, d), lambda i, e: (e, 0, 0))
    bspec = pl.BlockSpec((1, 1, d), lambda i, e: (e, 0, 0))
    return pl.pallas_call(
        functools.partial(_moe_kernel, t),
        grid=(n // tm, N_EXPERTS),
        in_specs=[tok(d), tok(N_EXPERTS), tok(d),
                  pl.BlockSpec(mod.shape, lambda i, e: (0, 0, 0)),
                  wspec, bspec, wspec, bspec, wspec, bspec,
                  pl.BlockSpec((1, d), lambda i, e: (0, 0))],
        out_specs=tok(d),
        out_shape=jax.ShapeDtypeStruct((n, d), F32),
        scratch_shapes=[pltpu.VMEM((tm, d), F32)],
        compiler_params=_params("parallel", "arbitrary"),
        name="moe",
    )(xn2, comb, x1, mod, wg, bg, wu, bu, wd, bd, g_final)


def _rope_tables(t, pos0):
    half = RET_QK_DIM // 2
    inv = ROPE_BASE ** (-jnp.arange(0, RET_QK_DIM, 2, dtype=F32) / RET_QK_DIM)
    pos = jnp.arange(t, dtype=F32) + float(pos0)
    ang = pos[:, None] * inv[None, :]
    assert ang.shape == (t, half)
    return jnp.cos(ang), jnp.sin(ang)


def _trunk(x, mod, pos0, conv_cache, s0, wts):
    bsz, t, d = x.shape
    cos, sin = _rope_tables(t, pos0)
    q, k, v, gs, cb, u, sga, sgb = _in_proj(x, mod, wts["g1"], cos, sin, wts["w_in"])
    yc, conv_new = _conv(u, cb, conv_cache, wts["conv_w"], wts["w_conv_out"])
    x1, xn2, s_new = _ret(q, k, v, gs, yc, sga, sgb, x, mod, s0,
                          wts["w_ret_out"], wts["w_mix_out"], wts["g2"])
    xn2 = xn2.reshape(bsz * t, d)
    comb = _router(xn2, wts["w_router"], wts["b_router"])
    y = _moe(xn2, comb, x1.reshape(bsz * t, d), mod, t, wts["w_gate"], wts["b_gate"],
             wts["w_up"], wts["b_up"], wts["w_down"], wts["b_down"], wts["g_final"])
    return y.reshape(bsz, t, d), conv_new[None], s_new[None]


def kernel(x_prompt, x_sample, c_prompt, c_sample, cache_conv, state_ret, w_ada, b_ada, g_norm1,
           w_in, conv_w, w_ret_out, w_conv_out, w_mix_out, g_norm2, w_router, b_router,
           w_gate, b_gate, w_up, b_up, w_down, b_down, g_final):
    assert w_ada.shape[0] == 1, "single-layer model"
    d = D_MODEL
    bp = x_prompt.shape[0]
    bs = x_sample.shape[0]
    wts = {
        "g1": g_norm1[0].reshape(1, d),
        "w_in": w_in[0].astype(BF16),
        "conv_w": conv_w[0],
        "w_ret_out": w_ret_out[0].astype(BF16),
        "w_conv_out": w_conv_out[0].astype(BF16),
        "w_mix_out": w_mix_out[0].astype(BF16),
        "g2": g_norm2[0].reshape(1, d),
        "w_router": w_router[0].astype(BF16),
        "b_router": b_router[0],
        "w_gate": w_gate[0].astype(BF16),
        "b_gate": b_gate[0].reshape(N_EXPERTS, 1, d),
        "w_up": w_up[0].astype(BF16),
        "b_up": b_up[0].reshape(N_EXPERTS, 1, d),
        "w_down": w_down[0].astype(BF16),
        "b_down": b_down[0].reshape(N_EXPERTS, 1, d),
        "g_final": g_final.reshape(1, d),
    }
    mod = _ada(jnp.concatenate([c_prompt, c_sample], axis=0), w_ada[0], b_ada[0])
    mod = mod.reshape(bp + bs, 6, d)
    conv0 = jnp.zeros((bp, CONV_K - 1, d), F32)
    ret0 = jnp.zeros((bp, RET_HEADS, RET_QK_DIM, RET_V_DIM), F32)
    y_p, conv_p, ret_p = _trunk(x_prompt, mod[:bp], 0, conv0, ret0, wts)
    y_s, conv_s, ret_s = _trunk(x_sample, mod[bp:], PAST_LEN, cache_conv[0], state_ret[0], wts)
    return (y_p, y_s, conv_p, ret_p, conv_s, ret_s)
```

```python
import functools

import jax
import jax.numpy as jnp
import numpy as np
from jax import lax
from jax.experimental import pallas as pl
from jax.experimental.pallas import tpu as pltpu

D_MODEL = 1024
CHUNK = 64
RET_HEADS = 4
RET_QK_DIM = 256
RET_V_DIM = 512
RET_QK = RET_HEADS * RET_QK_DIM
RET_V = RET_HEADS * RET_V_DIM
CONV_K = 3
N_EXPERTS = 32
TOP_K = 4
SWIGLU_LIMIT = 7.0
SWIGLU_ALPHA = 1.702
ROPE_BASE = 10000.0
PAST_LEN = 4096
EPS = 1e-6
PROJ_WIDTH = 2 * RET_QK + 2 * RET_V + 5 * D_MODEL

V7X_VMEM_LIMIT_BYTES = 56 * 1024 * 1024
RET_BLOCK = 256
PROJ_ROWS = 256
TOK_TILE = 256
EXPERT_ROWS = 512

F32 = jnp.float32
BF16 = jnp.bfloat16


def _params(*sem):
    return pltpu.CompilerParams(dimension_semantics=sem, vmem_limit_bytes=V7X_VMEM_LIMIT_BYTES)


def _sigmoid(x):
    return 1.0 / (1.0 + jnp.exp(-x))


def _rms(x):
    return x * lax.rsqrt(jnp.mean(x * x, axis=-1, keepdims=True) + EPS)


def _ada_kernel(c_ref, w_ref, b_ref, o_ref):
    c = c_ref[...]
    s = c * _sigmoid(c)
    o_ref[...] = jnp.dot(s, w_ref[...], preferred_element_type=F32,
                         precision=lax.Precision.HIGHEST) + b_ref[...]


def _ada(c, w, b):
    n, d = c.shape
    width = w.shape[1]
    tn = 1024
    return pl.pallas_call(
        _ada_kernel,
        grid=(width // tn,),
        in_specs=[pl.BlockSpec((n, d), lambda j: (0, 0)),
                  pl.BlockSpec((d, tn), lambda j: (0, j)),
                  pl.BlockSpec((1, tn), lambda j: (0, j))],
        out_specs=pl.BlockSpec((n, tn), lambda j: (0, j)),
        out_shape=jax.ShapeDtypeStruct((n, width), F32),
        compiler_params=_params("arbitrary"),
        name="ada",
    )(c, w, b.reshape(1, width))


def _in_proj_kernel(x_ref, mod_ref, g1_ref, cos_ref, sin_ref, w_ref,
                    q_ref, k_ref, v_ref, gs_ref, cb_ref, u_ref, sga_ref, sgb_ref):
    x = x_ref[0]
    mod = mod_ref[0]
    xn = _rms(x) * g1_ref[...]
    xn = (xn * (1.0 + mod[1:2]) + mod[0:1]).astype(BF16)
    cos = cos_ref[...]
    sin = sin_ref[...]

    def proj(c0, width):
        return jnp.dot(xn, w_ref[:, c0:c0 + width], preferred_element_type=F32)

    def rope_store(acc, out_ref, scale):
        half = RET_QK_DIM // 2
        for h in range(RET_HEADS):
            a = acc[:, h * RET_QK_DIM: h * RET_QK_DIM + half]
            b = acc[:, h * RET_QK_DIM + half: (h + 1) * RET_QK_DIM]
            out_ref[0, :, h * RET_QK_DIM: h * RET_QK_DIM + half] = (
                (a * cos - b * sin) * scale).astype(BF16)
            out_ref[0, :, h * RET_QK_DIM + half: (h + 1) * RET_QK_DIM] = (
                (a * sin + b * cos) * scale).astype(BF16)

    c0 = 0
    rope_store(proj(c0, RET_QK), q_ref, 1.0)
    c0 += RET_QK
    rope_store(proj(c0, RET_QK), k_ref, RET_QK_DIM ** -0.5)
    c0 += RET_QK
    for j in range(RET_V // 1024):
        v_ref[0, :, j * 1024:(j + 1) * 1024] = proj(c0 + j * 1024, 1024).astype(BF16)
    c0 += RET_V
    for j in range(RET_V // 1024):
        g = proj(c0 + j * 1024, 1024)
        gs_ref[0, :, j * 1024:(j + 1) * 1024] = (g * _sigmoid(g)).astype(BF16)
    c0 += RET_V
    cb_ref[0] = proj(c0, D_MODEL).astype(BF16)
    c0 += D_MODEL
    cc = proj(c0, D_MODEL)
    c0 += D_MODEL
    u_ref[0] = cc * proj(c0, D_MODEL)
    c0 += D_MODEL
    sga_ref[0] = _sigmoid(proj(c0, D_MODEL)).astype(BF16)
    c0 += D_MODEL
    sgb_ref[0] = _sigmoid(proj(c0, D_MODEL)).astype(BF16)


def _in_proj(x, mod, g1, cos, sin, w_in_bf16):
    bsz, t, d = x.shape
    tm = min(PROJ_ROWS, t)
    row = lambda width: pl.BlockSpec((1, tm, width), lambda b, i: (b, i, 0))
    shp = lambda width, dt: jax.ShapeDtypeStruct((bsz, t, width), dt)
    return pl.pallas_call(
        _in_proj_kernel,
        grid=(bsz, t // tm),
        in_specs=[row(d),
                  pl.BlockSpec((1, 6, d), lambda b, i: (b, 0, 0)),
                  pl.BlockSpec((1, d), lambda b, i: (0, 0)),
                  pl.BlockSpec((tm, RET_QK_DIM // 2), lambda b, i: (i, 0)),
                  pl.BlockSpec((tm, RET_QK_DIM // 2), lambda b, i: (i, 0)),
                  pl.BlockSpec((d, PROJ_WIDTH), lambda b, i: (0, 0),
                               pipeline_mode=pl.Buffered(1))],
        out_specs=[row(RET_QK), row(RET_QK), row(RET_V), row(RET_V),
                   row(d), row(d), row(d), row(d)],
        out_shape=[shp(RET_QK, BF16), shp(RET_QK, BF16), shp(RET_V, BF16), shp(RET_V, BF16),
                   shp(d, BF16), shp(d, F32), shp(d, BF16), shp(d, BF16)],
        compiler_params=_params("parallel", "parallel"),
        name="in_proj",
    )(x, mod, g1, cos, sin, w_in_bf16)


def _conv_kernel(u_ref, cb_ref, cache_ref, cw_ref, w_ref, y_ref, new_ref, prev_ref):
    i = pl.program_id(1)

    @pl.when(i == 0)
    def _():
        prev_ref[...] = cache_ref[0]

    u = u_ref[0]
    tc = u.shape[0]
    prev = prev_ref[...]
    rows = lax.broadcasted_iota(jnp.int32, u.shape, 0)
    u1 = jnp.where(rows == 0, prev[1:2], pltpu.roll(u, 1, axis=0))
    u2 = jnp.where(rows == 0, prev[0:1], jnp.where(rows == 1, prev[1:2], pltpu.roll(u, 2, axis=0)))
    cw = cw_ref[...]
    conv = u2 * cw[0:1] + u1 * cw[1:2] + u * cw[2:3]
    z = (cb_ref[0].astype(F32) * conv).astype(BF16)
    y_ref[0] = jnp.dot(z, w_ref[...], preferred_element_type=F32).astype(BF16)
    tail = u_ref[0, tc - 2:tc, :]
    prev_ref[...] = tail
    new_ref[0] = tail


def _conv(u, cb, cache, conv_w, w_conv_out_bf16):
    bsz, t, d = u.shape
    tc = min(512, t)
    return pl.pallas_call(
        _conv_kernel,
        grid=(bsz, t // tc),
        in_specs=[pl.BlockSpec((1, tc, d), lambda b, i: (b, i, 0)),
                  pl.BlockSpec((1, tc, d), lambda b, i: (b, i, 0)),
                  pl.BlockSpec((1, CONV_K - 1, d), lambda b, i: (b, 0, 0)),
                  pl.BlockSpec((CONV_K, d), lambda b, i: (0, 0)),
                  pl.BlockSpec((d, d), lambda b, i: (0, 0))],
        out_specs=[pl.BlockSpec((1, tc, d), lambda b, i: (b, i, 0)),
                   pl.BlockSpec((1, CONV_K - 1, d), lambda b, i: (b, 0, 0))],
        out_shape=[jax.ShapeDtypeStruct((bsz, t, d), BF16),
                   jax.ShapeDtypeStruct((bsz, CONV_K - 1, d), F32)],
        scratch_shapes=[pltpu.VMEM((CONV_K - 1, d), F32)],
        compiler_params=_params("parallel", "arbitrary"),
        name="conv",
    )(u, cb, cache, conv_w, w_conv_out_bf16)


def _decay_tables(block):
    lg = np.log(1.0 - np.exp2(-5.0 - np.arange(RET_HEADS, dtype=np.float64)))
    idx = np.arange(block, dtype=np.float64)
    dist = np.abs(idx[:, None] - idx[None, :])
    visible = (idx[None, :] // CHUNK) <= (idx[:, None] // CHUNK)
    dmat = np.where(visible[None], np.exp(dist[None] * lg[:, None, None]), 0.0)
    qdec = np.exp((idx + 1.0)[None, :] * lg[:, None])
    kdec = np.exp((block - 1.0 - idx)[None, :] * lg[:, None])
    sdec = np.exp(block * lg)
    lanes = lambda a: jnp.asarray(np.broadcast_to(a[:, :, None], (RET_HEADS, block, 128)), F32)
    return jnp.asarray(dmat, F32), lanes(qdec), lanes(kdec), [float(s) for s in sdec]


def _ret_kernel(sdec, q_ref, k_ref, v_ref, gs_ref, yc_ref, sga_ref, sgb_ref, x_ref, mod_ref,
                s0_ref, dmat_ref, qdec_ref, kdec_ref, wro_ref, wmo_ref, g2_ref,
                x1_ref, xn2_ref, s_ref):
    i = pl.program_id(1)

    @pl.when(i == 0)
    def _():
        s_ref[...] = s0_ref[...]

    ls = q_ref.shape[1]
    yret = jnp.zeros((ls, D_MODEL), F32)
    for h in range(RET_HEADS):
        qh = q_ref[0, :, h * RET_QK_DIM:(h + 1) * RET_QK_DIM]
        kh = k_ref[0, :, h * RET_QK_DIM:(h + 1) * RET_QK_DIM]
        vh = v_ref[0, :, h * RET_V_DIM:(h + 1) * RET_V_DIM]
        state = s_ref[0, h]
        scores = lax.dot_general(qh, kh, (((1,), (1,)), ((), ())), preferred_element_type=F32)
        p = (scores * dmat_ref[h]).astype(BF16)
        o = jnp.dot(p, vh, preferred_element_type=F32)
        qdec = jnp.concatenate([qdec_ref[h]] * (RET_V_DIM // 128), axis=1)
        o = o + jnp.dot(qh, state.astype(BF16), preferred_element_type=F32) * qdec
        kdec = jnp.concatenate([kdec_ref[h]] * (RET_QK_DIM // 128), axis=1)
        kd = (kh.astype(F32) * kdec).astype(BF16)
        s_ref[0, h] = sdec[h] * state + lax.dot_general(
            kd, vh, (((0,), (0,)), ((), ())), preferred_element_type=F32)
        og = (gs_ref[0, :, h * RET_V_DIM:(h + 1) * RET_V_DIM].astype(F32) * _rms(o)).astype(BF16)
        yret = yret + jnp.dot(og, wro_ref[h * RET_V_DIM:(h + 1) * RET_V_DIM, :],
                              preferred_element_type=F32)
    hmix = (sga_ref[0].astype(F32) * yc_ref[0].astype(F32) + sgb_ref[0].astype(F32) * yret)
    mix = jnp.dot(hmix.astype(BF16), wmo_ref[...], preferred_element_type=F32)
    mod = mod_ref[0]
    x1 = x_ref[0] + mod[2:3] * mix
    x1_ref[0] = x1
    xn2 = _rms(x1) * g2_ref[...]
    xn2_ref[0] = xn2 * (1.0 + mod[4:5]) + mod[3:4]


def _ret(q, k, v, gs, yc, sga, sgb, x, mod, s0, w_ret_out_bf16, w_mix_out_bf16, g2):
    bsz, t, d = x.shape
    ls = min(RET_BLOCK, t)
    dmat, qdec, kdec, sdec = _decay_tables(ls)
    row = lambda width: pl.BlockSpec((1, ls, width), lambda b, i: (b, i, 0))
    const = lambda shape: pl.BlockSpec(shape, lambda b, i: (0,) * len(shape))
    state_spec = pl.BlockSpec((1, RET_HEADS, RET_QK_DIM, RET_V_DIM), lambda b, i: (b, 0, 0, 0))
    return pl.pallas_call(
        functools.partial(_ret_kernel, sdec),
        grid=(bsz, t // ls),
        in_specs=[row(RET_QK), row(RET_QK), row(RET_V), row(RET_V), row(d), row(d), row(d), row(d),
                  pl.BlockSpec((1, 6, d), lambda b, i: (b, 0, 0)),
                  state_spec,
                  const((RET_HEADS, ls, ls)), const((RET_HEADS, ls, 128)),
                  const((RET_HEADS, ls, 128)),
                  const((RET_V, d)), const((d, d)), const((1, d))],
        out_specs=[row(d), row(d), state_spec],
        out_shape=[jax.ShapeDtypeStruct((bsz, t, d), F32),
                   jax.ShapeDtypeStruct((bsz, t, d), F32),
                   jax.ShapeDtypeStruct(s0.shape, F32)],
        compiler_params=_params("parallel", "arbitrary"),
        name="ret",
    )(q, k, v, gs, yc, sga, sgb, x, mod, s0, dmat, qdec, kdec, w_ret_out_bf16, w_mix_out_bf16, g2)


def _two_stream_specs(n_p, n_s, tm, width):
    tiles_p = n_p // tm
    last_p = tiles_p - 1
    return (pl.BlockSpec((tm, width), lambda i: (jnp.minimum(i, last_p), 0)),
            pl.BlockSpec((tm, width), lambda i: (jnp.maximum(i - tiles_p, 0), 0)))


def _router_kernel(tiles_p, xp_ref, xs_ref, w_ref, b_ref, idx_ref, prob_ref, rank_ref, cnt_ref,
                   run_ref):
    i = pl.program_id(0)

    @pl.when(i == 0)
    def _():
        run_ref[...] = jnp.zeros_like(run_ref)

    xn = jnp.where(i < tiles_p, xp_ref[...], xs_ref[...]).astype(BF16)
    logits = jnp.dot(xn, w_ref[...], preferred_element_type=F32) + b_ref[...]
    tm = logits.shape[0]
    lane = lax.broadcasted_iota(jnp.int32, logits.shape, 1)
    col = lax.broadcasted_iota(jnp.int32, (tm, TOP_K), 1)
    work = logits
    sels, vals = [], []
    idx = jnp.zeros((tm, TOP_K), jnp.int32)
    for k in range(TOP_K):
        m = jnp.max(work, axis=-1, keepdims=True)
        first = jnp.min(jnp.where(work == m, lane, N_EXPERTS), axis=-1, keepdims=True)
        sel = lane == first
        sels.append(sel)
        vals.append(m)
        idx = jnp.where(col == k, first, idx)
        work = jnp.where(sel, -jnp.inf, work)
    exps = [jnp.exp(v - vals[0]) for v in vals]
    denom = exps[0] + exps[1] + exps[2] + exps[3]
    prob = jnp.zeros((tm, TOP_K), F32)
    for k in range(TOP_K):
        prob = jnp.where(col == k, exps[k] / denom, prob)
    member = jnp.zeros(logits.shape, F32)
    for sel in sels:
        member = member + jnp.where(sel, 1.0, 0.0)
    r = lax.broadcasted_iota(jnp.int32, (tm, tm), 0)
    c = lax.broadcasted_iota(jnp.int32, (tm, tm), 1)
    earlier = jnp.where(r > c, 1.0, 0.0).astype(BF16)
    before = jnp.dot(earlier, member.astype(BF16), preferred_element_type=F32) + run_ref[...]
    rank = jnp.zeros((tm, TOP_K), jnp.int32)
    for k in range(TOP_K):
        rk = jnp.sum(jnp.where(sels[k], before, 0.0), axis=-1, keepdims=True)
        rank = jnp.where(col == k, rk.astype(jnp.int32), rank)
    run_ref[...] += jnp.sum(member, axis=0, keepdims=True)
    idx_ref[...] = idx
    prob_ref[...] = prob
    rank_ref[...] = rank
    cnt_ref[...] = run_ref[...].astype(jnp.int32)


def _router(xn2_p, xn2_s, w_router_bf16, b_router):
    (n_p, d), n_s = xn2_p.shape, xn2_s.shape[0]
    tm = TOK_TILE
    n = n_p + n_s
    spec_p, spec_s = _two_stream_specs(n_p, n_s, tm, d)
    tok4 = pl.BlockSpec((tm, TOP_K), lambda i: (i, 0))
    return pl.pallas_call(
        functools.partial(_router_kernel, n_p // tm),
        grid=(n // tm,),
        in_specs=[spec_p, spec_s,
                  pl.BlockSpec((d, N_EXPERTS), lambda i: (0, 0)),
                  pl.BlockSpec((1, N_EXPERTS), lambda i: (0, 0))],
        out_specs=[tok4, tok4, tok4, pl.BlockSpec((1, N_EXPERTS), lambda i: (0, 0))],
        out_shape=[jax.ShapeDtypeStruct((n, TOP_K), jnp.int32),
                   jax.ShapeDtypeStruct((n, TOP_K), F32),
                   jax.ShapeDtypeStruct((n, TOP_K), jnp.int32),
                   jax.ShapeDtypeStruct((1, N_EXPERTS), jnp.int32)],
        scratch_shapes=[pltpu.VMEM((1, N_EXPERTS), F32)],
        compiler_params=_params("arbitrary"),
        name="router",
    )(xn2_p, xn2_s, w_router_bf16, b_router.reshape(1, N_EXPERTS))


def _dispatch_kernel(tiles_p, pos_hbm, xp_ref, xs_ref, out_hbm, pos_smem, row_buf, sem_pos, sem_rows):
    i = pl.program_id(0)
    tm = row_buf.shape[0]
    pos_copy = pltpu.make_async_copy(pos_hbm.at[pl.ds(i * (tm * TOP_K), tm * TOP_K)], pos_smem,
                                     sem_pos)
    pos_copy.start()
    row_buf[...] = jnp.where(i < tiles_p, xp_ref[...], xs_ref[...])
    pos_copy.wait()

    def row_copy(r, dst):
        return pltpu.make_async_copy(row_buf.at[pl.ds(r, 1)], out_hbm.at[pl.ds(dst, 1)], sem_rows)

    def issue(r, carry):
        for k in range(TOP_K):
            row_copy(r, pos_smem[r * TOP_K + k]).start()
        return carry

    lax.fori_loop(0, tm, issue, 0, unroll=8)
    for _ in range(TOP_K):
        pltpu.make_async_copy(row_buf, out_hbm.at[pl.ds(0, tm)], sem_rows).wait()


def _dispatch(xn2_p, xn2_s, pos_flat):
    (n_p, d), n_s = xn2_p.shape, xn2_s.shape[0]
    tm = TOK_TILE
    n = n_p + n_s
    spec_p, spec_s = _two_stream_specs(n_p, n_s, tm, d)
    return pl.pallas_call(
        functools.partial(_dispatch_kernel, n_p // tm),
        grid=(n // tm,),
        in_specs=[pl.BlockSpec(memory_space=pl.ANY), spec_p, spec_s],
        out_specs=pl.BlockSpec(memory_space=pl.ANY),
        out_shape=jax.ShapeDtypeStruct((n * TOP_K, d), F32),
        scratch_shapes=[pltpu.SMEM((tm * TOP_K,), jnp.int32),
                        pltpu.VMEM((tm, d), F32),
                        pltpu.SemaphoreType.DMA(()),
                        pltpu.SemaphoreType.DMA(())],
        compiler_params=_params("arbitrary"),
        name="dispatch",
    )(pos_flat, xn2_p, xn2_s)


def _visit_schedule(counts, n_rows, tm):
    n_visits = n_rows // tm + N_EXPERTS - 1
    off = jnp.concatenate([jnp.zeros((1,), jnp.int32), jnp.cumsum(counts, dtype=jnp.int32)])
    first_tile = off[:-1] // tm
    last_tile = (off[1:] - 1) // tm
    per_expert = jnp.where(counts > 0, last_tile - first_tile + 1, 0)
    ends = jnp.cumsum(per_expert, dtype=jnp.int32)
    total = ends[-1]
    v = jnp.minimum(jnp.arange(n_visits, dtype=jnp.int32), total - 1)
    expert = jnp.minimum(jnp.searchsorted(ends, v, side="right").astype(jnp.int32), N_EXPERTS - 1)
    tile = first_tile[expert] + (v - (ends[expert] - per_expert[expert]))
    live = jnp.arange(n_visits, dtype=jnp.int32) < total
    lo = jnp.clip(off[expert] - tile * tm, 0, tm)
    hi = jnp.where(live, jnp.clip(off[expert + 1] - tile * tm, 0, tm), lo)
    first = jnp.concatenate([jnp.ones((1,), jnp.int32), (tile[1:] != tile[:-1]).astype(jnp.int32)])
    return tile, expert, lo, hi, first, off


def _expert_kernel(tile_ref, expert_ref, lo_ref, hi_ref, first_ref,
                   x_ref, wg_ref, bg_ref, wu_ref, bu_ref, wd_ref, bd_ref, y_ref):
    v = pl.program_id(0)
    lo = lo_ref[v]
    hi = hi_ref[v]

    @pl.when(hi > lo)
    def _():
        xn = x_ref[...].astype(BF16)
        gl = jnp.minimum(jnp.dot(xn, wg_ref[0], preferred_element_type=F32) + bg_ref[0],
                         SWIGLU_LIMIT)
        lin = jnp.clip(jnp.dot(xn, wu_ref[0], preferred_element_type=F32) + bu_ref[0],
                       -SWIGLU_LIMIT, SWIGLU_LIMIT)
        act = gl * _sigmoid(SWIGLU_ALPHA * gl) * (lin + 1.0)
        out = jnp.dot(act.astype(BF16), wd_ref[0], preferred_element_type=F32) + bd_ref[0]
        row = lax.broadcasted_iota(jnp.int32, out.shape, 0)
        mine = (row >= lo) & (row < hi)

        @pl.when(first_ref[v] == 1)
        def _():
            y_ref[...] = jnp.where(mine, out, 0.0)

        @pl.when(first_ref[v] == 0)
        def _():
            y_ref[...] = jnp.where(mine, out, y_ref[...])


def _experts(xs, schedule, wg, bg, wu, bu, wd, bd):
    n_rows, d = xs.shape
    tm = EXPERT_ROWS
    tile, expert, lo, hi, first = schedule
    rows = pl.BlockSpec((tm, d), lambda v, tile, expert, lo, hi, first: (tile[v], 0))
    wspec = pl.BlockSpec((1, d, d), lambda v, tile, expert, lo, hi, first: (expert[v], 0, 0))
    bspec = pl.BlockSpec((1, 1, d), lambda v, tile, expert, lo, hi, first: (expert[v], 0, 0))
    return pl.pallas_call(
        _expert_kernel,
        grid_spec=pltpu.PrefetchScalarGridSpec(
            num_scalar_prefetch=5,
            grid=(tile.shape[0],),
            in_specs=[rows, wspec, bspec, wspec, bspec, wspec, bspec],
            out_specs=rows),
        out_shape=jax.ShapeDtypeStruct((n_rows, d), F32),
        compiler_params=_params("arbitrary"),
        name="experts",
    )(tile, expert, lo, hi, first, xs, wg, bg, wu, bu, wd, bd)


def _combine_kernel(tiles_p, n_p, t_p, b_p, t_s, pos_hbm, ys_hbm, prob_ref, x1p_ref, x1s_ref,
                    mod_ref, gf_ref, yp_ref, ysm_ref, pos_smem, gath, sem_pos, sem_rows):
    i = pl.program_id(0)
    tm = gath.shape[1]
    pos_copy = pltpu.make_async_copy(pos_hbm.at[pl.ds(i * (tm * TOP_K), tm * TOP_K)], pos_smem,
                                     sem_pos)
    pos_copy.start()
    pos_copy.wait()

    def issue(r, carry):
        for k in range(TOP_K):
            src = pos_smem[r * TOP_K + k]
            pltpu.make_async_copy(ys_hbm.at[pl.ds(src, 1)], gath.at[k, pl.ds(r, 1)],
                                  sem_rows).start()
        return carry

    lax.fori_loop(0, tm, issue, 0, unroll=8)
    for k in range(TOP_K):
        pltpu.make_async_copy(ys_hbm.at[pl.ds(0, tm)], gath.at[k], sem_rows).wait()

    prob = prob_ref[...]
    moe = prob[:, 0:1] * gath[0]
    for k in range(1, TOP_K):
        moe = moe + prob[:, k:k + 1] * gath[k]
    x1 = jnp.where(i < tiles_p, x1p_ref[...], x1s_ref[...])
    gate_rows = []
    for j in range(tm // CHUNK):
        n0 = i * tm + j * CHUNK
        b = jnp.where(n0 < n_p, n0 // t_p, b_p + (n0 - n_p) // t_s)
        gate_rows.append(jnp.broadcast_to(mod_ref[b, 5:6, :], (CHUNK, D_MODEL)))
    x2 = x1 + jnp.concatenate(gate_rows, axis=0) * moe
    y = _rms(x2) * gf_ref[...]

    @pl.when(i < tiles_p)
    def _():
        yp_ref[...] = y

    @pl.when(i >= tiles_p)
    def _():
        ysm_ref[...] = y


def _combine(ys, pos_flat, prob, x1_p, x1_s, mod, t_p, t_s, g_final):
    (n_p, d), n_s = x1_p.shape, x1_s.shape[0]
    tm = TOK_TILE
    n = n_p + n_s
    assert t_p % CHUNK == 0 and t_s % CHUNK == 0 and tm % CHUNK == 0
    spec_p, spec_s = _two_stream_specs(n_p, n_s, tm, d)
    return pl.pallas_call(
        functools.partial(_combine_kernel, n_p // tm, n_p, t_p, n_p // t_p, t_s),
        grid=(n // tm,),
        in_specs=[pl.BlockSpec(memory_space=pl.ANY), pl.BlockSpec(memory_space=pl.ANY),
                  pl.BlockSpec((tm, TOP_K), lambda i: (i, 0)), spec_p, spec_s,
                  pl.BlockSpec(mod.shape, lambda i: (0, 0, 0)),
                  pl.BlockSpec((1, d), lambda i: (0, 0))],
        out_specs=[spec_p, spec_s],
        out_shape=[jax.ShapeDtypeStruct((n_p, d), F32), jax.ShapeDtypeStruct((n_s, d), F32)],
        scratch_shapes=[pltpu.SMEM((tm * TOP_K,), jnp.int32),
                        pltpu.VMEM((TOP_K, tm, d), F32),
                        pltpu.SemaphoreType.DMA(()),
                        pltpu.SemaphoreType.DMA(())],
        compiler_params=_params("arbitrary"),
        name="combine",
    )(pos_flat, ys, prob, x1_p, x1_s, mod, g_final)


def _rope_tables(t, pos0):
    half = RET_QK_DIM // 2
    inv = ROPE_BASE ** (-jnp.arange(0, RET_QK_DIM, 2, dtype=F32) / RET_QK_DIM)
    pos = jnp.arange(t, dtype=F32) + float(pos0)
    ang = pos[:, None] * inv[None, :]
    assert ang.shape == (t, half)
    return jnp.cos(ang), jnp.sin(ang)


def _mixer(x, mod, pos0, conv_cache, s0, wts):
    bsz, t, d = x.shape
    cos, sin = _rope_tables(t, pos0)
    q, k, v, gs, cb, u, sga, sgb = _in_proj(x, mod, wts["g1"], cos, sin, wts["w_in"])
    yc, conv_new = _conv(u, cb, conv_cache, wts["conv_w"], wts["w_conv_out"])
    x1, xn2, s_new = _ret(q, k, v, gs, yc, sga, sgb, x, mod, s0,
                          wts["w_ret_out"], wts["w_mix_out"], wts["g2"])
    return x1.reshape(bsz * t, d), xn2.reshape(bsz * t, d), conv_new[None], s_new[None]


def _moe(x1_p, xn2_p, x1_s, xn2_s, mod, t_p, t_s, wts):
    n = x1_p.shape[0] + x1_s.shape[0]
    idx, prob, rank, counts = _router(xn2_p, xn2_s, wts["w_router"], wts["b_router"])
    counts = counts.reshape(N_EXPERTS)
    *schedule, off = _visit_schedule(counts, n * TOP_K, EXPERT_ROWS)
    pos_flat = (off[idx] + rank).reshape(n * TOP_K)
    xs = _dispatch(xn2_p, xn2_s, pos_flat)
    ys = _experts(xs, schedule, wts["w_gate"], wts["b_gate"], wts["w_up"], wts["b_up"],
                  wts["w_down"], wts["b_down"])
    return _combine(ys, pos_flat, prob, x1_p, x1_s, mod, t_p, t_s, wts["g_final"])


def kernel(x_prompt, x_sample, c_prompt, c_sample, cache_conv, state_ret, w_ada, b_ada, g_norm1,
           w_in, conv_w, w_ret_out, w_conv_out, w_mix_out, g_norm2, w_router, b_router,
           w_gate, b_gate, w_up, b_up, w_down, b_down, g_final):
    assert w_ada.shape[0] == 1, "single-layer model"
    d = D_MODEL
    bp = x_prompt.shape[0]
    bs = x_sample.shape[0]
    wts = {
        "g1": g_norm1[0].reshape(1, d),
        "w_in": w_in[0].astype(BF16),
        "conv_w": conv_w[0],
        "w_ret_out": w_ret_out[0].astype(BF16),
        "w_conv_out": w_conv_out[0].astype(BF16),
        "w_mix_out": w_mix_out[0].astype(BF16),
        "g2": g_norm2[0].reshape(1, d),
        "w_router": w_router[0].astype(BF16),
        "b_router": b_router[0],
        "w_gate": w_gate[0].astype(BF16),
        "b_gate": b_gate[0].reshape(N_EXPERTS, 1, d),
        "w_up": w_up[0].astype(BF16),
        "b_up": b_up[0].reshape(N_EXPERTS, 1, d),
        "w_down": w_down[0].astype(BF16),
        "b_down": b_down[0].reshape(N_EXPERTS, 1, d),
        "g_final": g_final.reshape(1, d),
    }
    mod = _ada(jnp.concatenate([c_prompt, c_sample], axis=0), w_ada[0], b_ada[0])
    mod = mod.reshape(bp + bs, 6, d)
    conv0 = jnp.zeros((bp, CONV_K - 1, d), F32)
    ret0 = jnp.zeros((bp, RET_HEADS, RET_QK_DIM, RET_V_DIM), F32)
    t_p, t_s = x_prompt.shape[1], x_sample.shape[1]
    x1_p, xn2_p, conv_p, ret_p = _mixer(x_prompt, mod[:bp], 0, conv0, ret0, wts)
    x1_s, xn2_s, conv_s, ret_s = _mixer(x_sample, mod[bp:], PAST_LEN, cache_conv[0], state_ret[0],
                                        wts)
    y_p, y_s = _moe(x1_p, xn2_p, x1_s, xn2_s, mod, t_p, t_s, wts)
    return (y_p.reshape(x_prompt.shape), y_s.reshape(x_sample.shape), conv_p, ret_p, conv_s, ret_s)
```

```python
import functools

import jax
import jax.numpy as jnp
import numpy as np
from jax import lax
from jax.experimental import pallas as pl
from jax.experimental.pallas import tpu as pltpu

D_MODEL = 1024
CHUNK = 64
RET_HEADS = 4
RET_QK_DIM = 256
RET_V_DIM = 512
RET_QK = RET_HEADS * RET_QK_DIM
RET_V = RET_HEADS * RET_V_DIM
CONV_K = 3
N_EXPERTS = 32
TOP_K = 4
SWIGLU_LIMIT = 7.0
SWIGLU_ALPHA = 1.702
ROPE_BASE = 10000.0
PAST_LEN = 4096
EPS = 1e-6
PROJ_WIDTH = 2 * RET_QK + 2 * RET_V + 5 * D_MODEL

V7X_VMEM_LIMIT_BYTES = 56 * 1024 * 1024
RET_BLOCK = 256
PROJ_ROWS = 256
TOK_TILE = 256
EXPERT_ROWS = 512

F32 = jnp.float32
BF16 = jnp.bfloat16


def _params(*sem):
    return pltpu.CompilerParams(dimension_semantics=sem, vmem_limit_bytes=V7X_VMEM_LIMIT_BYTES)


def _sigmoid(x):
    return 1.0 / (1.0 + jnp.exp(-x))


def _rms(x):
    return x * lax.rsqrt(jnp.mean(x * x, axis=-1, keepdims=True) + EPS)


def _ada_kernel(c_ref, w_ref, b_ref, o_ref):
    c = c_ref[...]
    s = c * _sigmoid(c)
    o_ref[...] = jnp.dot(s, w_ref[...], preferred_element_type=F32,
                         precision=lax.Precision.HIGHEST) + b_ref[...]


def _ada(c, w, b):
    n, d = c.shape
    width = w.shape[1]
    tn = 1024
    return pl.pallas_call(
        _ada_kernel,
        grid=(width // tn,),
        in_specs=[pl.BlockSpec((n, d), lambda j: (0, 0)),
                  pl.BlockSpec((d, tn), lambda j: (0, j)),
                  pl.BlockSpec((1, tn), lambda j: (0, j))],
        out_specs=pl.BlockSpec((n, tn), lambda j: (0, j)),
        out_shape=jax.ShapeDtypeStruct((n, width), F32),
        compiler_params=_params("arbitrary"),
        name="ada",
    )(c, w, b.reshape(1, width))


def _in_proj_kernel(x_ref, mod_ref, g1_ref, cos_ref, sin_ref, w_ref,
                    q_ref, k_ref, v_ref, gs_ref, cb_ref, u_ref, sga_ref, sgb_ref):
    x = x_ref[0]
    mod = mod_ref[0]
    xn = _rms(x) * g1_ref[...]
    xn = (xn * (1.0 + mod[1:2]) + mod[0:1]).astype(BF16)
    cos = cos_ref[...]
    sin = sin_ref[...]

    def proj(c0, width):
        return jnp.dot(xn, w_ref[:, c0:c0 + width], preferred_element_type=F32)

    def rope_store(acc, out_ref, scale):
        half = RET_QK_DIM // 2
        for h in range(RET_HEADS):
            a = acc[:, h * RET_QK_DIM: h * RET_QK_DIM + half]
            b = acc[:, h * RET_QK_DIM + half: (h + 1) * RET_QK_DIM]
            out_ref[0, :, h * RET_QK_DIM: h * RET_QK_DIM + half] = (
                (a * cos - b * sin) * scale).astype(BF16)
            out_ref[0, :, h * RET_QK_DIM + half: (h + 1) * RET_QK_DIM] = (
                (a * sin + b * cos) * scale).astype(BF16)

    c0 = 0
    rope_store(proj(c0, RET_QK), q_ref, 1.0)
    c0 += RET_QK
    rope_store(proj(c0, RET_QK), k_ref, RET_QK_DIM ** -0.5)
    c0 += RET_QK
    for j in range(RET_V // 1024):
        v_ref[0, :, j * 1024:(j + 1) * 1024] = proj(c0 + j * 1024, 1024).astype(BF16)
    c0 += RET_V
    for j in range(RET_V // 1024):
        g = proj(c0 + j * 1024, 1024)
        gs_ref[0, :, j * 1024:(j + 1) * 1024] = (g * _sigmoid(g)).astype(BF16)
    c0 += RET_V
    cb_ref[0] = proj(c0, D_MODEL).astype(BF16)
    c0 += D_MODEL
    cc = proj(c0, D_MODEL)
    c0 += D_MODEL
    u_ref[0] = cc * proj(c0, D_MODEL)
    c0 += D_MODEL
    sga_ref[0] = _sigmoid(proj(c0, D_MODEL)).astype(BF16)
    c0 += D_MODEL
    sgb_ref[0] = _sigmoid(proj(c0, D_MODEL)).astype(BF16)


def _in_proj(x, mod, g1, cos, sin, w_in_bf16):
    bsz, t, d = x.shape
    tm = min(PROJ_ROWS, t)
    row = lambda width: pl.BlockSpec((1, tm, width), lambda b, i: (b, i, 0))
    shp = lambda width, dt: jax.ShapeDtypeStruct((bsz, t, width), dt)
    return pl.pallas_call(
        _in_proj_kernel,
        grid=(bsz, t // tm),
        in_specs=[row(d),
                  pl.BlockSpec((1, 6, d), lambda b, i: (b, 0, 0)),
                  pl.BlockSpec((1, d), lambda b, i: (0, 0)),
                  pl.BlockSpec((tm, RET_QK_DIM // 2), lambda b, i: (i, 0)),
                  pl.BlockSpec((tm, RET_QK_DIM // 2), lambda b, i: (i, 0)),
                  pl.BlockSpec((d, PROJ_WIDTH), lambda b, i: (0, 0),
                               pipeline_mode=pl.Buffered(1))],
        out_specs=[row(RET_QK), row(RET_QK), row(RET_V), row(RET_V),
                   row(d), row(d), row(d), row(d)],
        out_shape=[shp(RET_QK, BF16), shp(RET_QK, BF16), shp(RET_V, BF16), shp(RET_V, BF16),
                   shp(d, BF16), shp(d, F32), shp(d, BF16), shp(d, BF16)],
        compiler_params=_params("parallel", "parallel"),
        name="in_proj",
    )(x, mod, g1, cos, sin, w_in_bf16)


def _conv_kernel(u_ref, cb_ref, cache_ref, cw_ref, w_ref, y_ref, new_ref, prev_ref):
    i = pl.program_id(1)

    @pl.when(i == 0)
    def _():
        prev_ref[...] = cache_ref[0]

    u = u_ref[0]
    tc = u.shape[0]
    prev = prev_ref[...]
    rows = lax.broadcasted_iota(jnp.int32, u.shape, 0)
    u1 = jnp.where(rows == 0, prev[1:2], pltpu.roll(u, 1, axis=0))
    u2 = jnp.where(rows == 0, prev[0:1], jnp.where(rows == 1, prev[1:2], pltpu.roll(u, 2, axis=0)))
    cw = cw_ref[...]
    conv = u2 * cw[0:1] + u1 * cw[1:2] + u * cw[2:3]
    z = (cb_ref[0].astype(F32) * conv).astype(BF16)
    y_ref[0] = jnp.dot(z, w_ref[...], preferred_element_type=F32).astype(BF16)
    tail = u_ref[0, tc - 2:tc, :]
    prev_ref[...] = tail
    new_ref[0] = tail


def _conv(u, cb, cache, conv_w, w_conv_out_bf16):
    bsz, t, d = u.shape
    tc = min(512, t)
    return pl.pallas_call(
        _conv_kernel,
        grid=(bsz, t // tc),
        in_specs=[pl.BlockSpec((1, tc, d), lambda b, i: (b, i, 0)),
                  pl.BlockSpec((1, tc, d), lambda b, i: (b, i, 0)),
                  pl.BlockSpec((1, CONV_K - 1, d), lambda b, i: (b, 0, 0)),
                  pl.BlockSpec((CONV_K, d), lambda b, i: (0, 0)),
                  pl.BlockSpec((d, d), lambda b, i: (0, 0))],
        out_specs=[pl.BlockSpec((1, tc, d), lambda b, i: (b, i, 0)),
                   pl.BlockSpec((1, CONV_K - 1, d), lambda b, i: (b, 0, 0))],
        out_shape=[jax.ShapeDtypeStruct((bsz, t, d), BF16),
                   jax.ShapeDtypeStruct((bsz, CONV_K - 1, d), F32)],
        scratch_shapes=[pltpu.VMEM((CONV_K - 1, d), F32)],
        compiler_params=_params("parallel", "arbitrary"),
        name="conv",
    )(u, cb, cache, conv_w, w_conv_out_bf16)


def _decay_tables(block):
    lg = np.log(1.0 - np.exp2(-5.0 - np.arange(RET_HEADS, dtype=np.float64)))
    idx = np.arange(block, dtype=np.float64)
    dist = np.abs(idx[:, None] - idx[None, :])
    visible = (idx[None, :] // CHUNK) <= (idx[:, None] // CHUNK)
    dmat = np.where(visible[None], np.exp(dist[None] * lg[:, None, None]), 0.0)
    qdec = np.exp((idx + 1.0)[None, :] * lg[:, None])
    kdec = np.exp((block - 1.0 - idx)[None, :] * lg[:, None])
    sdec = np.exp(block * lg)
    lanes = lambda a: jnp.asarray(np.broadcast_to(a[:, :, None], (RET_HEADS, block, 128)), F32)
    return jnp.asarray(dmat, F32), lanes(qdec), lanes(kdec), [float(s) for s in sdec]


def _ret_kernel(sdec, q_ref, k_ref, v_ref, gs_ref, yc_ref, sga_ref, sgb_ref, x_ref, mod_ref,
                s0_ref, dmat_ref, qdec_ref, kdec_ref, wro_ref, wmo_ref, g2_ref,
                x1_ref, xn2_ref, s_ref):
    i = pl.program_id(1)

    @pl.when(i == 0)
    def _():
        s_ref[...] = s0_ref[...]

    ls = q_ref.shape[1]
    yret = jnp.zeros((ls, D_MODEL), F32)
    for h in range(RET_HEADS):
        qh = q_ref[0, :, h * RET_QK_DIM:(h + 1) * RET_QK_DIM]
        kh = k_ref[0, :, h * RET_QK_DIM:(h + 1) * RET_QK_DIM]
        vh = v_ref[0, :, h * RET_V_DIM:(h + 1) * RET_V_DIM]
        state = s_ref[0, h]
        scores = lax.dot_general(qh, kh, (((1,), (1,)), ((), ())), preferred_element_type=F32)
        p = (scores * dmat_ref[h]).astype(BF16)
        o = jnp.dot(p, vh, preferred_element_type=F32)
        qdec = jnp.concatenate([qdec_ref[h]] * (RET_V_DIM // 128), axis=1)
        o = o + jnp.dot(qh, state.astype(BF16), preferred_element_type=F32) * qdec
        kdec = jnp.concatenate([kdec_ref[h]] * (RET_QK_DIM // 128), axis=1)
        kd = (kh.astype(F32) * kdec).astype(BF16)
        s_ref[0, h] = sdec[h] * state + lax.dot_general(
            kd, vh, (((0,), (0,)), ((), ())), preferred_element_type=F32)
        og = (gs_ref[0, :, h * RET_V_DIM:(h + 1) * RET_V_DIM].astype(F32) * _rms(o)).astype(BF16)
        yret = yret + jnp.dot(og, wro_ref[h * RET_V_DIM:(h + 1) * RET_V_DIM, :],
                              preferred_element_type=F32)
    hmix = (sga_ref[0].astype(F32) * yc_ref[0].astype(F32) + sgb_ref[0].astype(F32) * yret)
    mix = jnp.dot(hmix.astype(BF16), wmo_ref[...], preferred_element_type=F32)
    mod = mod_ref[0]
    x1 = x_ref[0] + mod[2:3] * mix
    x1_ref[0] = x1
    xn2 = _rms(x1) * g2_ref[...]
    xn2_ref[0] = xn2 * (1.0 + mod[4:5]) + mod[3:4]


def _ret(q, k, v, gs, yc, sga, sgb, x, mod, s0, w_ret_out_bf16, w_mix_out_bf16, g2):
    bsz, t, d = x.shape
    ls = min(RET_BLOCK, t)
    dmat, qdec, kdec, sdec = _decay_tables(ls)
    row = lambda width: pl.BlockSpec((1, ls, width), lambda b, i: (b, i, 0))
    const = lambda shape: pl.BlockSpec(shape, lambda b, i: (0,) * len(shape))
    state_spec = pl.BlockSpec((1, RET_HEADS, RET_QK_DIM, RET_V_DIM), lambda b, i: (b, 0, 0, 0))
    return pl.pallas_call(
        functools.partial(_ret_kernel, sdec),
        grid=(bsz, t // ls),
        in_specs=[row(RET_QK), row(RET_QK), row(RET_V), row(RET_V), row(d), row(d), row(d), row(d),
                  pl.BlockSpec((1, 6, d), lambda b, i: (b, 0, 0)),
                  state_spec,
                  const((RET_HEADS, ls, ls)), const((RET_HEADS, ls, 128)),
                  const((RET_HEADS, ls, 128)),
                  const((RET_V, d)), const((d, d)), const((1, d))],
        out_specs=[row(d), row(d), state_spec],
        out_shape=[jax.ShapeDtypeStruct((bsz, t, d), F32),
                   jax.ShapeDtypeStruct((bsz, t, d), F32),
                   jax.ShapeDtypeStruct(s0.shape, F32)],
        compiler_params=_params("parallel", "arbitrary"),
        name="ret",
    )(q, k, v, gs, yc, sga, sgb, x, mod, s0, dmat, qdec, kdec, w_ret_out_bf16, w_mix_out_bf16, g2)


def _two_stream_specs(n_p, n_s, tm, width):
    tiles_p = n_p // tm
    last_p = tiles_p - 1
    return (pl.BlockSpec((tm, width), lambda i: (jnp.minimum(i, last_p), 0)),
            pl.BlockSpec((tm, width), lambda i: (jnp.maximum(i - tiles_p, 0), 0)))


def _router_kernel(tiles_p, xp_ref, xs_ref, w_ref, b_ref, idx_ref, prob_ref, rank_ref, cnt_ref,
                   run_ref):
    i = pl.program_id(0)

    @pl.when(i == 0)
    def _():
        run_ref[...] = jnp.zeros_like(run_ref)

    xn = jnp.where(i < tiles_p, xp_ref[...], xs_ref[...]).astype(BF16)
    logits = jnp.dot(xn, w_ref[...], preferred_element_type=F32) + b_ref[...]
    tm = logits.shape[0]
    lane = lax.broadcasted_iota(jnp.int32, logits.shape, 1)
    col = lax.broadcasted_iota(jnp.int32, (tm, TOP_K), 1)
    work = logits
    sels, vals = [], []
    idx = jnp.zeros((tm, TOP_K), jnp.int32)
    for k in range(TOP_K):
        m = jnp.max(work, axis=-1, keepdims=True)
        first = jnp.min(jnp.where(work == m, lane, N_EXPERTS), axis=-1, keepdims=True)
        sel = lane == first
        sels.append(sel)
        vals.append(m)
        idx = jnp.where(col == k, first, idx)
        work = jnp.where(sel, -jnp.inf, work)
    exps = [jnp.exp(v - vals[0]) for v in vals]
    denom = exps[0] + exps[1] + exps[2] + exps[3]
    prob = jnp.zeros((tm, TOP_K), F32)
    for k in range(TOP_K):
        prob = jnp.where(col == k, exps[k] / denom, prob)
    member = jnp.zeros(logits.shape, F32)
    for sel in sels:
        member = member + jnp.where(sel, 1.0, 0.0)
    r = lax.broadcasted_iota(jnp.int32, (tm, tm), 0)
    c = lax.broadcasted_iota(jnp.int32, (tm, tm), 1)
    earlier = jnp.where(r > c, 1.0, 0.0).astype(BF16)
    before = jnp.dot(earlier, member.astype(BF16), preferred_element_type=F32) + run_ref[...]
    rank = jnp.zeros((tm, TOP_K), jnp.int32)
    for k in range(TOP_K):
        rk = jnp.sum(jnp.where(sels[k], before, 0.0), axis=-1, keepdims=True)
        rank = jnp.where(col == k, rk.astype(jnp.int32), rank)
    run_ref[...] += jnp.sum(member, axis=0, keepdims=True)
    idx_ref[...] = idx
    prob_ref[...] = prob
    rank_ref[...] = rank
    cnt_ref[...] = run_ref[...].astype(jnp.int32)


def _router(xn2_p, xn2_s, w_router_bf16, b_router):
    (n_p, d), n_s = xn2_p.shape, xn2_s.shape[0]
    tm = TOK_TILE
    n = n_p + n_s
    spec_p, spec_s = _two_stream_specs(n_p, n_s, tm, d)
    tok4 = pl.BlockSpec((tm, TOP_K), lambda i: (i, 0))
    return pl.pallas_call(
        functools.partial(_router_kernel, n_p // tm),
        grid=(n // tm,),
        in_specs=[spec_p, spec_s,
                  pl.BlockSpec((d, N_EXPERTS), lambda i: (0, 0)),
                  pl.BlockSpec((1, N_EXPERTS), lambda i: (0, 0))],
        out_specs=[tok4, tok4, tok4, pl.BlockSpec((1, N_EXPERTS), lambda i: (0, 0))],
        out_shape=[jax.ShapeDtypeStruct((n, TOP_K), jnp.int32),
                   jax.ShapeDtypeStruct((n, TOP_K), F32),
                   jax.ShapeDtypeStruct((n, TOP_K), jnp.int32),
                   jax.ShapeDtypeStruct((1, N_EXPERTS), jnp.int32)],
        scratch_shapes=[pltpu.VMEM((1, N_EXPERTS), F32)],
        compiler_params=_params("arbitrary"),
        name="router",
    )(xn2_p, xn2_s, w_router_bf16, b_router.reshape(1, N_EXPERTS))


def _dispatch_kernel(tiles_p, pos_hbm, xp_ref, xs_ref, out_hbm, pos_smem, row_buf, sem_pos, sem_rows):
    i = pl.program_id(0)
    n = pl.num_programs(0)
    tm = row_buf.shape[1]
    slot = i % 2

    def pos_copy(step, s):
        return pltpu.make_async_copy(pos_hbm.at[pl.ds(step * (tm * TOP_K), tm * TOP_K)],
                                     pos_smem.at[s], sem_pos.at[s])

    def drain(s):
        for _ in range(TOP_K):
            pltpu.make_async_copy(row_buf.at[s], out_hbm.at[pl.ds(0, tm)], sem_rows.at[s]).wait()

    @pl.when(i == 0)
    def _():
        pos_copy(0, 0).start()

    @pl.when(i + 1 < n)
    def _():
        pos_copy(i + 1, 1 - slot).start()

    @pl.when(i >= 2)
    def _():
        drain(slot)

    row_buf[slot] = jnp.where(i < tiles_p, xp_ref[...], xs_ref[...])
    pos_copy(i, slot).wait()

    def issue(r, carry):
        for k in range(TOP_K):
            dst = pos_smem[slot, r * TOP_K + k]
            pltpu.make_async_copy(row_buf.at[slot, pl.ds(r, 1)], out_hbm.at[pl.ds(dst, 1)],
                                  sem_rows.at[slot]).start(priority=k % 2)
        return carry

    lax.fori_loop(0, tm, issue, 0, unroll=8)

    @pl.when(i == n - 1)
    def _():
        drain(slot)

    @pl.when((i == n - 1) & (n >= 2))
    def _():
        drain(1 - slot)


def _dispatch(xn2_p, xn2_s, pos_flat):
    (n_p, d), n_s = xn2_p.shape, xn2_s.shape[0]
    tm = TOK_TILE
    n = n_p + n_s
    spec_p, spec_s = _two_stream_specs(n_p, n_s, tm, d)
    return pl.pallas_call(
        functools.partial(_dispatch_kernel, n_p // tm),
        grid=(n // tm,),
        in_specs=[pl.BlockSpec(memory_space=pl.ANY), spec_p, spec_s],
        out_specs=pl.BlockSpec(memory_space=pl.ANY),
        out_shape=jax.ShapeDtypeStruct((n * TOP_K, d), F32),
        scratch_shapes=[pltpu.SMEM((2, tm * TOP_K), jnp.int32),
                        pltpu.VMEM((2, tm, d), F32),
                        pltpu.SemaphoreType.DMA((2,)),
                        pltpu.SemaphoreType.DMA((2,))],
        compiler_params=_params("arbitrary"),
        name="dispatch",
    )(pos_flat, xn2_p, xn2_s)


def _visit_schedule(counts, n_rows, tm):
    n_visits = n_rows // tm + N_EXPERTS - 1
    off = jnp.concatenate([jnp.zeros((1,), jnp.int32), jnp.cumsum(counts, dtype=jnp.int32)])
    first_tile = off[:-1] // tm
    last_tile = (off[1:] - 1) // tm
    per_expert = jnp.where(counts > 0, last_tile - first_tile + 1, 0)
    ends = jnp.cumsum(per_expert, dtype=jnp.int32)
    total = ends[-1]
    v = jnp.minimum(jnp.arange(n_visits, dtype=jnp.int32), total - 1)
    expert = jnp.sum((ends[None, :] <= v[:, None]).astype(jnp.int32), axis=1)
    mine = expert[:, None] == jnp.arange(N_EXPERTS, dtype=jnp.int32)[None, :]
    pick = lambda table: jnp.sum(jnp.where(mine, table[None, :], 0), axis=1)
    tile = pick(first_tile) + (v - pick(ends - per_expert))
    live = jnp.arange(n_visits, dtype=jnp.int32) < total
    lo = jnp.clip(pick(off[:-1]) - tile * tm, 0, tm)
    hi = jnp.where(live, jnp.clip(pick(off[1:]) - tile * tm, 0, tm), lo)
    first = jnp.concatenate([jnp.ones((1,), jnp.int32), (tile[1:] != tile[:-1]).astype(jnp.int32)])
    return tile, expert, lo, hi, first, off


def _expert_kernel(tile_ref, expert_ref, lo_ref, hi_ref, first_ref,
                   x_ref, wg_ref, bg_ref, wu_ref, bu_ref, wd_ref, bd_ref, y_ref):
    v = pl.program_id(0)
    lo = lo_ref[v]
    hi = hi_ref[v]

    @pl.when(hi > lo)
    def _():
        xn = x_ref[...].astype(BF16)
        gl = jnp.minimum(jnp.dot(xn, wg_ref[0], preferred_element_type=F32) + bg_ref[0],
                         SWIGLU_LIMIT)
        lin = jnp.clip(jnp.dot(xn, wu_ref[0], preferred_element_type=F32) + bu_ref[0],
                       -SWIGLU_LIMIT, SWIGLU_LIMIT)
        act = gl * _sigmoid(SWIGLU_ALPHA * gl) * (lin + 1.0)
        out = jnp.dot(act.astype(BF16), wd_ref[0], preferred_element_type=F32) + bd_ref[0]
        row = lax.broadcasted_iota(jnp.int32, out.shape, 0)
        mine = (row >= lo) & (row < hi)

        @pl.when(first_ref[v] == 1)
        def _():
            y_ref[...] = jnp.where(mine, out, 0.0)

        @pl.when(first_ref[v] == 0)
        def _():
            y_ref[...] = jnp.where(mine, out, y_ref[...])


def _experts(xs, schedule, wg, bg, wu, bu, wd, bd):
    n_rows, d = xs.shape
    tm = EXPERT_ROWS
    tile, expert, lo, hi, first = schedule
    rows = pl.BlockSpec((tm, d), lambda v, tile, expert, lo, hi, first: (tile[v], 0))
    wspec = pl.BlockSpec((1, d, d), lambda v, tile, expert, lo, hi, first: (expert[v], 0, 0))
    bspec = pl.BlockSpec((1, 1, d), lambda v, tile, expert, lo, hi, first: (expert[v], 0, 0))
    return pl.pallas_call(
        _expert_kernel,
        grid_spec=pltpu.PrefetchScalarGridSpec(
            num_scalar_prefetch=5,
            grid=(tile.shape[0],),
            in_specs=[rows, wspec, bspec, wspec, bspec, wspec, bspec],
            out_specs=rows),
        out_shape=jax.ShapeDtypeStruct((n_rows, d), F32),
        compiler_params=_params("arbitrary"),
        name="experts",
    )(tile, expert, lo, hi, first, xs, wg, bg, wu, bu, wd, bd)


def _combine_kernel(tiles_p, n_p, t_p, b_p, t_s, pos_hbm, ys_hbm, prob_ref, x1p_ref, x1s_ref,
                    mod_ref, gf_ref, yp_ref, ysm_ref, pos_smem, gath, sem_pos, sem_rows):
    i = pl.program_id(0)
    n = pl.num_programs(0)
    tm = gath.shape[2]
    slot = i % 2

    def pos_copy(step):
        s = step % 3
        return pltpu.make_async_copy(pos_hbm.at[pl.ds(step * (tm * TOP_K), tm * TOP_K)],
                                     pos_smem.at[s], sem_pos.at[s])

    def gather_tile(step):
        ps, gs = step % 3, step % 2

        def issue(r, carry):
            for k in range(TOP_K):
                src = pos_smem[ps, r * TOP_K + k]
                pltpu.make_async_copy(ys_hbm.at[pl.ds(src, 1)], gath.at[gs, k, pl.ds(r, 1)],
                                      sem_rows.at[gs]).start(priority=k % 2)
            return carry

        lax.fori_loop(0, tm, issue, 0, unroll=8)

    @pl.when(i == 0)
    def _():
        pos_copy(0).start()
        pos_copy(0).wait()
        gather_tile(0)

    @pl.when((i == 0) & (n > 1))
    def _():
        pos_copy(1).start()

    @pl.when(i + 2 < n)
    def _():
        pos_copy(i + 2).start()

    @pl.when(i + 1 < n)
    def _():
        pos_copy(i + 1).wait()
        gather_tile(i + 1)

    for k in range(TOP_K):
        pltpu.make_async_copy(ys_hbm.at[pl.ds(0, tm)], gath.at[slot, k], sem_rows.at[slot]).wait()

    prob = prob_ref[...]
    moe = prob[:, 0:1] * gath[slot, 0]
    for k in range(1, TOP_K):
        moe = moe + prob[:, k:k + 1] * gath[slot, k]
    x1 = jnp.where(i < tiles_p, x1p_ref[...], x1s_ref[...])
    gate_rows = []
    for j in range(tm // CHUNK):
        n0 = i * tm + j * CHUNK
        b = jnp.where(n0 < n_p, n0 // t_p, b_p + (n0 - n_p) // t_s)
        gate_rows.append(jnp.broadcast_to(mod_ref[b, 5:6, :], (CHUNK, D_MODEL)))
    x2 = x1 + jnp.concatenate(gate_rows, axis=0) * moe
    y = _rms(x2) * gf_ref[...]

    @pl.when(i < tiles_p)
    def _():
        yp_ref[...] = y

    @pl.when(i >= tiles_p)
    def _():
        ysm_ref[...] = y


def _combine(ys, pos_flat, prob, x1_p, x1_s, mod, t_p, t_s, g_final):
    (n_p, d), n_s = x1_p.shape, x1_s.shape[0]
    tm = TOK_TILE
    n = n_p + n_s
    assert t_p % CHUNK == 0 and t_s % CHUNK == 0 and tm % CHUNK == 0
    spec_p, spec_s = _two_stream_specs(n_p, n_s, tm, d)
    return pl.pallas_call(
        functools.partial(_combine_kernel, n_p // tm, n_p, t_p, n_p // t_p, t_s),
        grid=(n // tm,),
        in_specs=[pl.BlockSpec(memory_space=pl.ANY), pl.BlockSpec(memory_space=pl.ANY),
                  pl.BlockSpec((tm, TOP_K), lambda i: (i, 0)), spec_p, spec_s,
                  pl.BlockSpec(mod.shape, lambda i: (0, 0, 0)),
                  pl.BlockSpec((1, d), lambda i: (0, 0))],
        out_specs=[spec_p, spec_s],
        out_shape=[jax.ShapeDtypeStruct((n_p, d), F32), jax.ShapeDtypeStruct((n_s, d), F32)],
        scratch_shapes=[pltpu.SMEM((3, tm * TOP_K), jnp.int32),
                        pltpu.VMEM((2, TOP_K, tm, d), F32),
                        pltpu.SemaphoreType.DMA((3,)),
                        pltpu.SemaphoreType.DMA((2,))],
        compiler_params=_params("arbitrary"),
        name="combine",
    )(pos_flat, ys, prob, x1_p, x1_s, mod, g_final)


def _rope_tables(t, pos0):
    half = RET_QK_DIM // 2
    inv = ROPE_BASE ** (-jnp.arange(0, RET_QK_DIM, 2, dtype=F32) / RET_QK_DIM)
    pos = jnp.arange(t, dtype=F32) + float(pos0)
    ang = pos[:, None] * inv[None, :]
    assert ang.shape == (t, half)
    return jnp.cos(ang), jnp.sin(ang)


def _mixer(x, mod, pos0, conv_cache, s0, wts):
    bsz, t, d = x.shape
    cos, sin = _rope_tables(t, pos0)
    q, k, v, gs, cb, u, sga, sgb = _in_proj(x, mod, wts["g1"], cos, sin, wts["w_in"])
    yc, conv_new = _conv(u, cb, conv_cache, wts["conv_w"], wts["w_conv_out"])
    x1, xn2, s_new = _ret(q, k, v, gs, yc, sga, sgb, x, mod, s0,
                          wts["w_ret_out"], wts["w_mix_out"], wts["g2"])
    return x1.reshape(bsz * t, d), xn2.reshape(bsz * t, d), conv_new[None], s_new[None]


def _moe(x1_p, xn2_p, x1_s, xn2_s, mod, t_p, t_s, wts):
    n = x1_p.shape[0] + x1_s.shape[0]
    idx, prob, rank, counts = _router(xn2_p, xn2_s, wts["w_router"], wts["b_router"])
    counts = counts.reshape(N_EXPERTS)
    *schedule, off = _visit_schedule(counts, n * TOP_K, EXPERT_ROWS)
    experts = jnp.arange(N_EXPERTS, dtype=jnp.int32)
    base = jnp.sum(jnp.where(idx[..., None] == experts, off[:-1], 0), axis=-1)
    pos_flat = (base + rank).reshape(n * TOP_K)
    xs = _dispatch(xn2_p, xn2_s, pos_flat)
    ys = _experts(xs, schedule, wts["w_gate"], wts["b_gate"], wts["w_up"], wts["b_up"],
                  wts["w_down"], wts["b_down"])
    return _combine(ys, pos_flat, prob, x1_p, x1_s, mod, t_p, t_s, wts["g_final"])


def kernel(x_prompt, x_sample, c_prompt, c_sample, cache_conv, state_ret, w_ada, b_ada, g_norm1,
           w_in, conv_w, w_ret_out, w_conv_out, w_mix_out, g_norm2, w_router, b_router,
           w_gate, b_gate, w_up, b_up, w_down, b_down, g_final):
    assert w_ada.shape[0] == 1, "single-layer model"
    d = D_MODEL
    bp = x_prompt.shape[0]
    bs = x_sample.shape[0]
    wts = {
        "g1": g_norm1[0].reshape(1, d),
        "w_in": w_in[0].astype(BF16),
        "conv_w": conv_w[0],
        "w_ret_out": w_ret_out[0].astype(BF16),
        "w_conv_out": w_conv_out[0].astype(BF16),
        "w_mix_out": w_mix_out[0].astype(BF16),
        "g2": g_norm2[0].reshape(1, d),
        "w_router": w_router[0].astype(BF16),
        "b_router": b_router[0],
        "w_gate": w_gate[0].astype(BF16),
        "b_gate": b_gate[0].reshape(N_EXPERTS, 1, d),
        "w_up": w_up[0].astype(BF16),
        "b_up": b_up[0].reshape(N_EXPERTS, 1, d),
        "w_down": w_down[0].astype(BF16),
        "b_down": b_down[0].reshape(N_EXPERTS, 1, d),
        "g_final": g_final.reshape(1, d),
    }
    mod = _ada(jnp.concatenate([c_prompt, c_sample], axis=0), w_ada[0], b_ada[0])
    mod = mod.reshape(bp + bs, 6, d)
    conv0 = jnp.zeros((bp, CONV_K - 1, d), F32)
    ret0 = jnp.zeros((bp, RET_HEADS, RET_QK_DIM, RET_V_DIM), F32)
    t_p, t_s = x_prompt.shape[1], x_sample.shape[1]
    x1_p, xn2_p, conv_p, ret_p = _mixer(x_prompt, mod[:bp], 0, conv0, ret0, wts)
    x1_s, xn2_s, conv_s, ret_s = _mixer(x_sample, mod[bp:], PAST_LEN, cache_conv[0], state_ret[0],
                                        wts)
    y_p, y_s = _moe(x1_p, xn2_p, x1_s, xn2_s, mod, t_p, t_s, wts)
    return (y_p.reshape(x_prompt.shape), y_s.reshape(x_sample.shape), conv_p, ret_p, conv_s, ret_s)
```

```python
import functools

import jax
import jax.numpy as jnp
import numpy as np
from jax import lax
from jax.experimental import pallas as pl
from jax.experimental.pallas import tpu as pltpu

D_MODEL = 1024
CHUNK = 64
RET_HEADS = 4
RET_QK_DIM = 256
RET_V_DIM = 512
RET_QK = RET_HEADS * RET_QK_DIM
RET_V = RET_HEADS * RET_V_DIM
CONV_K = 3
N_EXPERTS = 32
TOP_K = 4
SWIGLU_LIMIT = 7.0
SWIGLU_ALPHA = 1.702
ROPE_BASE = 10000.0
PAST_LEN = 4096
EPS = 1e-6
PROJ_WIDTH = 2 * RET_QK + 2 * RET_V + 5 * D_MODEL

V7X_VMEM_LIMIT_BYTES = 56 * 1024 * 1024
RET_BLOCK = 256
PROJ_ROWS = 256
TOK_TILE = 256
EXPERT_ROWS = 512
COMBINE_WINDOW = 64
BF16_SUBLANES = 16
VISIT_FIRST, VISIT_MERGE, VISIT_ZERO = 1, 0, 2

F32 = jnp.float32
BF16 = jnp.bfloat16


def _params(*sem):
    return pltpu.CompilerParams(dimension_semantics=sem, vmem_limit_bytes=V7X_VMEM_LIMIT_BYTES)


def _sigmoid(x):
    return 1.0 / (1.0 + jnp.exp(-x))


def _rms(x):
    return x * lax.rsqrt(jnp.mean(x * x, axis=-1, keepdims=True) + EPS)


def _ada_kernel(c_ref, w_ref, b_ref, o_ref):
    c = c_ref[...]
    s = c * _sigmoid(c)
    o_ref[...] = jnp.dot(s, w_ref[...], preferred_element_type=F32,
                         precision=lax.Precision.HIGHEST) + b_ref[...]


def _ada(c, w, b):
    n, d = c.shape
    width = w.shape[1]
    tn = 1024
    return pl.pallas_call(
        _ada_kernel,
        grid=(width // tn,),
        in_specs=[pl.BlockSpec((n, d), lambda j: (0, 0)),
                  pl.BlockSpec((d, tn), lambda j: (0, j)),
                  pl.BlockSpec((1, tn), lambda j: (0, j))],
        out_specs=pl.BlockSpec((n, tn), lambda j: (0, j)),
        out_shape=jax.ShapeDtypeStruct((n, width), F32),
        compiler_params=_params("arbitrary"),
        name="ada",
    )(c, w, b.reshape(1, width))


def _in_proj_kernel(x_ref, mod_ref, g1_ref, cos_ref, sin_ref, w_ref,
                    q_ref, k_ref, v_ref, gs_ref, cb_ref, u_ref, sga_ref, sgb_ref):
    x = x_ref[0]
    mod = mod_ref[0]
    xn = _rms(x) * g1_ref[...]
    xn = (xn * (1.0 + mod[1:2]) + mod[0:1]).astype(BF16)
    cos = cos_ref[...]
    sin = sin_ref[...]

    def proj(c0, width):
        return jnp.dot(xn, w_ref[:, c0:c0 + width], preferred_element_type=F32)

    def rope_store(acc, out_ref, scale):
        half = RET_QK_DIM // 2
        for h in range(RET_HEADS):
            a = acc[:, h * RET_QK_DIM: h * RET_QK_DIM + half]
            b = acc[:, h * RET_QK_DIM + half: (h + 1) * RET_QK_DIM]
            out_ref[0, :, h * RET_QK_DIM: h * RET_QK_DIM + half] = (
                (a * cos - b * sin) * scale).astype(BF16)
            out_ref[0, :, h * RET_QK_DIM + half: (h + 1) * RET_QK_DIM] = (
                (a * sin + b * cos) * scale).astype(BF16)

    c0 = 0
    rope_store(proj(c0, RET_QK), q_ref, 1.0)
    c0 += RET_QK
    rope_store(proj(c0, RET_QK), k_ref, RET_QK_DIM ** -0.5)
    c0 += RET_QK
    for j in range(RET_V // 1024):
        v_ref[0, :, j * 1024:(j + 1) * 1024] = proj(c0 + j * 1024, 1024).astype(BF16)
    c0 += RET_V
    for j in range(RET_V // 1024):
        g = proj(c0 + j * 1024, 1024)
        gs_ref[0, :, j * 1024:(j + 1) * 1024] = (g * _sigmoid(g)).astype(BF16)
    c0 += RET_V
    cb_ref[0] = proj(c0, D_MODEL).astype(BF16)
    c0 += D_MODEL
    cc = proj(c0, D_MODEL)
    c0 += D_MODEL
    u_ref[0] = cc * proj(c0, D_MODEL)
    c0 += D_MODEL
    sga_ref[0] = _sigmoid(proj(c0, D_MODEL)).astype(BF16)
    c0 += D_MODEL
    sgb_ref[0] = _sigmoid(proj(c0, D_MODEL)).astype(BF16)


def _in_proj(x, mod, g1, cos, sin, w_in_bf16):
    bsz, t, d = x.shape
    tm = min(PROJ_ROWS, t)
    row = lambda width: pl.BlockSpec((1, tm, width), lambda b, i: (b, i, 0))
    shp = lambda width, dt: jax.ShapeDtypeStruct((bsz, t, width), dt)
    return pl.pallas_call(
        _in_proj_kernel,
        grid=(bsz, t // tm),
        in_specs=[row(d),
                  pl.BlockSpec((1, 6, d), lambda b, i: (b, 0, 0)),
                  pl.BlockSpec((1, d), lambda b, i: (0, 0)),
                  pl.BlockSpec((tm, RET_QK_DIM // 2), lambda b, i: (i, 0)),
                  pl.BlockSpec((tm, RET_QK_DIM // 2), lambda b, i: (i, 0)),
                  pl.BlockSpec((d, PROJ_WIDTH), lambda b, i: (0, 0),
                               pipeline_mode=pl.Buffered(1))],
        out_specs=[row(RET_QK), row(RET_QK), row(RET_V), row(RET_V),
                   row(d), row(d), row(d), row(d)],
        out_shape=[shp(RET_QK, BF16), shp(RET_QK, BF16), shp(RET_V, BF16), shp(RET_V, BF16),
                   shp(d, BF16), shp(d, F32), shp(d, BF16), shp(d, BF16)],
        compiler_params=_params("parallel", "parallel"),
        name="in_proj",
    )(x, mod, g1, cos, sin, w_in_bf16)


def _conv_kernel(u_ref, cb_ref, cache_ref, cw_ref, w_ref, y_ref, new_ref, prev_ref):
    i = pl.program_id(1)

    @pl.when(i == 0)
    def _():
        prev_ref[...] = cache_ref[0]

    u = u_ref[0]
    tc = u.shape[0]
    prev = prev_ref[...]
    rows = lax.broadcasted_iota(jnp.int32, u.shape, 0)
    u1 = jnp.where(rows == 0, prev[1:2], pltpu.roll(u, 1, axis=0))
    u2 = jnp.where(rows == 0, prev[0:1], jnp.where(rows == 1, prev[1:2], pltpu.roll(u, 2, axis=0)))
    cw = cw_ref[...]
    conv = u2 * cw[0:1] + u1 * cw[1:2] + u * cw[2:3]
    z = (cb_ref[0].astype(F32) * conv).astype(BF16)
    y_ref[0] = jnp.dot(z, w_ref[...], preferred_element_type=F32).astype(BF16)
    tail = u_ref[0, tc - 2:tc, :]
    prev_ref[...] = tail
    new_ref[0] = tail


def _conv(u, cb, cache, conv_w, w_conv_out_bf16):
    bsz, t, d = u.shape
    tc = min(512, t)
    return pl.pallas_call(
        _conv_kernel,
        grid=(bsz, t // tc),
        in_specs=[pl.BlockSpec((1, tc, d), lambda b, i: (b, i, 0)),
                  pl.BlockSpec((1, tc, d), lambda b, i: (b, i, 0)),
                  pl.BlockSpec((1, CONV_K - 1, d), lambda b, i: (b, 0, 0)),
                  pl.BlockSpec((CONV_K, d), lambda b, i: (0, 0)),
                  pl.BlockSpec((d, d), lambda b, i: (0, 0))],
        out_specs=[pl.BlockSpec((1, tc, d), lambda b, i: (b, i, 0)),
                   pl.BlockSpec((1, CONV_K - 1, d), lambda b, i: (b, 0, 0))],
        out_shape=[jax.ShapeDtypeStruct((bsz, t, d), BF16),
                   jax.ShapeDtypeStruct((bsz, CONV_K - 1, d), F32)],
        scratch_shapes=[pltpu.VMEM((CONV_K - 1, d), F32)],
        compiler_params=_params("parallel", "arbitrary"),
        name="conv",
    )(u, cb, cache, conv_w, w_conv_out_bf16)


def _decay_tables(block):
    lg = np.log(1.0 - np.exp2(-5.0 - np.arange(RET_HEADS, dtype=np.float64)))
    idx = np.arange(block, dtype=np.float64)
    dist = np.abs(idx[:, None] - idx[None, :])
    visible = (idx[None, :] // CHUNK) <= (idx[:, None] // CHUNK)
    dmat = np.where(visible[None], np.exp(dist[None] * lg[:, None, None]), 0.0)
    qdec = np.exp((idx + 1.0)[None, :] * lg[:, None])
    kdec = np.exp((block - 1.0 - idx)[None, :] * lg[:, None])
    sdec = np.exp(block * lg)
    lanes = lambda a: jnp.asarray(np.broadcast_to(a[:, :, None], (RET_HEADS, block, 128)), F32)
    return jnp.asarray(dmat, F32), lanes(qdec), lanes(kdec), [float(s) for s in sdec]


def _ret_kernel(sdec, q_ref, k_ref, v_ref, gs_ref, yc_ref, sga_ref, sgb_ref, x_ref, mod_ref,
                s0_ref, dmat_ref, qdec_ref, kdec_ref, wro_ref, wmo_ref, g2_ref,
                x1_ref, xn2_ref, s_ref):
    i = pl.program_id(1)

    @pl.when(i == 0)
    def _():
        s_ref[...] = s0_ref[...]

    ls = q_ref.shape[1]
    yret = jnp.zeros((ls, D_MODEL), F32)
    for h in range(RET_HEADS):
        qh = q_ref[0, :, h * RET_QK_DIM:(h + 1) * RET_QK_DIM]
        kh = k_ref[0, :, h * RET_QK_DIM:(h + 1) * RET_QK_DIM]
        vh = v_ref[0, :, h * RET_V_DIM:(h + 1) * RET_V_DIM]
        state = s_ref[0, h]
        scores = lax.dot_general(qh, kh, (((1,), (1,)), ((), ())), preferred_element_type=F32)
        p = (scores * dmat_ref[h]).astype(BF16)
        o = jnp.dot(p, vh, preferred_element_type=F32)
        qdec = jnp.concatenate([qdec_ref[h]] * (RET_V_DIM // 128), axis=1)
        o = o + jnp.dot(qh, state.astype(BF16), preferred_element_type=F32) * qdec
        kdec = jnp.concatenate([kdec_ref[h]] * (RET_QK_DIM // 128), axis=1)
        kd = (kh.astype(F32) * kdec).astype(BF16)
        s_ref[0, h] = sdec[h] * state + lax.dot_general(
            kd, vh, (((0,), (0,)), ((), ())), preferred_element_type=F32)
        og = (gs_ref[0, :, h * RET_V_DIM:(h + 1) * RET_V_DIM].astype(F32) * _rms(o)).astype(BF16)
        yret = yret + jnp.dot(og, wro_ref[h * RET_V_DIM:(h + 1) * RET_V_DIM, :],
                              preferred_element_type=F32)
    hmix = (sga_ref[0].astype(F32) * yc_ref[0].astype(F32) + sgb_ref[0].astype(F32) * yret)
    mix = jnp.dot(hmix.astype(BF16), wmo_ref[...], preferred_element_type=F32)
    mod = mod_ref[0]
    x1 = x_ref[0] + mod[2:3] * mix
    x1_ref[0] = x1
    xn2 = _rms(x1) * g2_ref[...]
    xn2_ref[0] = xn2 * (1.0 + mod[4:5]) + mod[3:4]


def _ret(q, k, v, gs, yc, sga, sgb, x, mod, s0, w_ret_out_bf16, w_mix_out_bf16, g2):
    bsz, t, d = x.shape
    ls = min(RET_BLOCK, t)
    dmat, qdec, kdec, sdec = _decay_tables(ls)
    row = lambda width: pl.BlockSpec((1, ls, width), lambda b, i: (b, i, 0))
    const = lambda shape: pl.BlockSpec(shape, lambda b, i: (0,) * len(shape))
    state_spec = pl.BlockSpec((1, RET_HEADS, RET_QK_DIM, RET_V_DIM), lambda b, i: (b, 0, 0, 0))
    return pl.pallas_call(
        functools.partial(_ret_kernel, sdec),
        grid=(bsz, t // ls),
        in_specs=[row(RET_QK), row(RET_QK), row(RET_V), row(RET_V), row(d), row(d), row(d), row(d),
                  pl.BlockSpec((1, 6, d), lambda b, i: (b, 0, 0)),
                  state_spec,
                  const((RET_HEADS, ls, ls)), const((RET_HEADS, ls, 128)),
                  const((RET_HEADS, ls, 128)),
                  const((RET_V, d)), const((d, d)), const((1, d))],
        out_specs=[row(d), row(d), state_spec],
        out_shape=[jax.ShapeDtypeStruct((bsz, t, d), F32),
                   jax.ShapeDtypeStruct((bsz, t, d), F32),
                   jax.ShapeDtypeStruct(s0.shape, F32)],
        compiler_params=_params("parallel", "arbitrary"),
        name="ret",
    )(q, k, v, gs, yc, sga, sgb, x, mod, s0, dmat, qdec, kdec, w_ret_out_bf16, w_mix_out_bf16, g2)


def _two_stream_specs(n_p, n_s, tm, width):
    tiles_p = n_p // tm
    last_p = tiles_p - 1
    return (pl.BlockSpec((tm, width), lambda i: (jnp.minimum(i, last_p), 0)),
            pl.BlockSpec((tm, width), lambda i: (jnp.maximum(i - tiles_p, 0), 0)))


def _router_kernel(tiles_p, xp_ref, xs_ref, w_ref, b_ref, idx_ref, prob_ref, rank_ref, base_ref,
                   cnt_ref, run_ref):
    i = pl.program_id(0)

    @pl.when(i == 0)
    def _():
        run_ref[...] = jnp.zeros_like(run_ref)

    base_ref[0] = run_ref[...].astype(jnp.int32)

    xn = jnp.where(i < tiles_p, xp_ref[...], xs_ref[...]).astype(BF16)
    logits = jnp.dot(xn, w_ref[...], preferred_element_type=F32) + b_ref[...]
    tm = logits.shape[0]
    lane = lax.broadcasted_iota(jnp.int32, logits.shape, 1)
    col = lax.broadcasted_iota(jnp.int32, (tm, TOP_K), 1)
    work = logits
    sels, vals = [], []
    idx = jnp.zeros((tm, TOP_K), jnp.int32)
    for k in range(TOP_K):
        m = jnp.max(work, axis=-1, keepdims=True)
        first = jnp.min(jnp.where(work == m, lane, N_EXPERTS), axis=-1, keepdims=True)
        sel = lane == first
        sels.append(sel)
        vals.append(m)
        idx = jnp.where(col == k, first, idx)
        work = jnp.where(sel, -jnp.inf, work)
    exps = [jnp.exp(v - vals[0]) for v in vals]
    denom = exps[0] + exps[1] + exps[2] + exps[3]
    prob = jnp.zeros((tm, TOP_K), F32)
    for k in range(TOP_K):
        prob = jnp.where(col == k, exps[k] / denom, prob)
    member = jnp.zeros(logits.shape, F32)
    for sel in sels:
        member = member + jnp.where(sel, 1.0, 0.0)
    r = lax.broadcasted_iota(jnp.int32, (tm, tm), 0)
    c = lax.broadcasted_iota(jnp.int32, (tm, tm), 1)
    earlier = jnp.where(r > c, 1.0, 0.0).astype(BF16)
    before = jnp.dot(earlier, member.astype(BF16), preferred_element_type=F32) + run_ref[...]
    rank = jnp.zeros((tm, TOP_K), jnp.int32)
    for k in range(TOP_K):
        rk = jnp.sum(jnp.where(sels[k], before, 0.0), axis=-1, keepdims=True)
        rank = jnp.where(col == k, rk.astype(jnp.int32), rank)
    run_ref[...] += jnp.sum(member, axis=0, keepdims=True)
    idx_ref[...] = idx
    prob_ref[...] = prob
    rank_ref[...] = rank
    cnt_ref[...] = run_ref[...].astype(jnp.int32)


def _router(xn2_p, xn2_s, w_router_bf16, b_router):
    (n_p, d), n_s = xn2_p.shape, xn2_s.shape[0]
    tm = TOK_TILE
    n = n_p + n_s
    spec_p, spec_s = _two_stream_specs(n_p, n_s, tm, d)
    tok4 = pl.BlockSpec((tm, TOP_K), lambda i: (i, 0))
    return pl.pallas_call(
        functools.partial(_router_kernel, n_p // tm),
        grid=(n // tm,),
        in_specs=[spec_p, spec_s,
                  pl.BlockSpec((d, N_EXPERTS), lambda i: (0, 0)),
                  pl.BlockSpec((1, N_EXPERTS), lambda i: (0, 0))],
        out_specs=[tok4, tok4, tok4,
                   pl.BlockSpec((1, 1, N_EXPERTS), lambda i: (i, 0, 0)),
                   pl.BlockSpec((1, N_EXPERTS), lambda i: (0, 0))],
        out_shape=[jax.ShapeDtypeStruct((n, TOP_K), jnp.int32),
                   jax.ShapeDtypeStruct((n, TOP_K), F32),
                   jax.ShapeDtypeStruct((n, TOP_K), jnp.int32),
                   jax.ShapeDtypeStruct((n // tm, 1, N_EXPERTS), jnp.int32),
                   jax.ShapeDtypeStruct((1, N_EXPERTS), jnp.int32)],
        scratch_shapes=[pltpu.VMEM((1, N_EXPERTS), F32)],
        compiler_params=_params("arbitrary"),
        name="router",
    )(xn2_p, xn2_s, w_router_bf16, b_router.reshape(1, N_EXPERTS))


def _dispatch_kernel(tiles_p, pos_hbm, xp_ref, xs_ref, out_hbm, pos_smem, row_buf, sem_pos, sem_rows):
    i = pl.program_id(0)
    n = pl.num_programs(0)
    tm = row_buf.shape[1]
    slot = i % 2

    def pos_copy(step, s):
        return pltpu.make_async_copy(pos_hbm.at[pl.ds(step * (tm * TOP_K), tm * TOP_K)],
                                     pos_smem.at[s], sem_pos.at[s])

    def drain(s):
        for _ in range(TOP_K):
            pltpu.make_async_copy(row_buf.at[s], out_hbm.at[pl.ds(0, tm)], sem_rows.at[s]).wait()

    @pl.when(i == 0)
    def _():
        pos_copy(0, 0).start()

    @pl.when(i + 1 < n)
    def _():
        pos_copy(i + 1, 1 - slot).start()

    @pl.when(i >= 2)
    def _():
        drain(slot)

    row_buf[slot] = jnp.where(i < tiles_p, xp_ref[...], xs_ref[...])
    pos_copy(i, slot).wait()

    def issue(r, carry):
        for k in range(TOP_K):
            dst = pos_smem[slot, r * TOP_K + k]
            pltpu.make_async_copy(row_buf.at[slot, pl.ds(r, 1)], out_hbm.at[pl.ds(dst, 1)],
                                  sem_rows.at[slot]).start(priority=k % 2)
        return carry

    lax.fori_loop(0, tm, issue, 0, unroll=8)

    @pl.when(i == n - 1)
    def _():
        drain(slot)

    @pl.when((i == n - 1) & (n >= 2))
    def _():
        drain(1 - slot)


def _dispatch(xn2_p, xn2_s, pos_flat):
    (n_p, d), n_s = xn2_p.shape, xn2_s.shape[0]
    tm = TOK_TILE
    n = n_p + n_s
    spec_p, spec_s = _two_stream_specs(n_p, n_s, tm, d)
    return pl.pallas_call(
        functools.partial(_dispatch_kernel, n_p // tm),
        grid=(n // tm,),
        in_specs=[pl.BlockSpec(memory_space=pl.ANY), spec_p, spec_s],
        out_specs=pl.BlockSpec(memory_space=pl.ANY),
        out_shape=jax.ShapeDtypeStruct((n * TOP_K, d), F32),
        scratch_shapes=[pltpu.SMEM((2, tm * TOP_K), jnp.int32),
                        pltpu.VMEM((2, tm, d), F32),
                        pltpu.SemaphoreType.DMA((2,)),
                        pltpu.SemaphoreType.DMA((2,))],
        compiler_params=_params("arbitrary"),
        name="dispatch",
    )(pos_flat, xn2_p, xn2_s)


def _visit_schedule(counts, n_rows, tm):
    n_tiles = n_rows // tm
    n_visits = n_tiles + N_EXPERTS
    off = jnp.concatenate([jnp.zeros((1,), jnp.int32), jnp.cumsum(counts, dtype=jnp.int32)])
    first_tile = off[:-1] // tm
    last_tile = (off[1:] - 1) // tm
    per_expert = jnp.where(counts > 0, last_tile - first_tile + 1, 0)
    ends = jnp.cumsum(per_expert, dtype=jnp.int32)
    total = ends[-1]
    step = jnp.arange(n_visits, dtype=jnp.int32)
    v = jnp.minimum(step, total - 1)
    expert = jnp.sum((ends[None, :] <= v[:, None]).astype(jnp.int32), axis=1)
    mine = expert[:, None] == jnp.arange(N_EXPERTS, dtype=jnp.int32)[None, :]
    pick = lambda table: jnp.sum(jnp.where(mine, table[None, :], 0), axis=1)
    live = step < total
    tile = jnp.where(live, pick(first_tile) + (v - pick(ends - per_expert)), n_tiles)
    lo = jnp.where(live, jnp.clip(pick(off[:-1]) - tile * tm, 0, tm), 0)
    hi = jnp.where(live, jnp.clip(pick(off[1:]) - tile * tm, 0, tm), 0)
    changed = jnp.concatenate([jnp.ones((1,), jnp.bool_), tile[1:] != tile[:-1]])
    mode = jnp.where(live, jnp.where(changed, VISIT_FIRST, VISIT_MERGE),
                     jnp.where(step == total, VISIT_ZERO, VISIT_MERGE)).astype(jnp.int32)
    return tile, expert, lo, hi, mode, off


def _expert_kernel(tile_ref, expert_ref, lo_ref, hi_ref, mode_ref,
                   x_ref, wg_ref, bg_ref, wu_ref, bu_ref, wd_ref, bd_ref, y_ref):
    v = pl.program_id(0)
    lo = lo_ref[v]
    hi = hi_ref[v]

    @pl.when(mode_ref[v] == VISIT_ZERO)
    def _():
        y_ref[...] = jnp.zeros_like(y_ref)

    @pl.when(hi > lo)
    def _():
        xn = x_ref[...].astype(BF16)
        gl = jnp.minimum(jnp.dot(xn, wg_ref[0], preferred_element_type=F32) + bg_ref[0],
                         SWIGLU_LIMIT)
        lin = jnp.clip(jnp.dot(xn, wu_ref[0], preferred_element_type=F32) + bu_ref[0],
                       -SWIGLU_LIMIT, SWIGLU_LIMIT)
        act = gl * _sigmoid(SWIGLU_ALPHA * gl) * (lin + 1.0)
        out = jnp.dot(act.astype(BF16), wd_ref[0], preferred_element_type=F32) + bd_ref[0]
        row = lax.broadcasted_iota(jnp.int32, out.shape, 0)
        mine = (row >= lo) & (row < hi)

        @pl.when(mode_ref[v] == VISIT_FIRST)
        def _():
            y_ref[...] = jnp.where(mine, out, 0.0).astype(y_ref.dtype)

        @pl.when(mode_ref[v] == VISIT_MERGE)
        def _():
            y_ref[...] = jnp.where(mine, out.astype(y_ref.dtype), y_ref[...])


def _experts(xs, schedule, wg, bg, wu, bu, wd, bd):
    n_rows, d = xs.shape
    tm = EXPERT_ROWS
    last_in = n_rows // tm - 1
    tile, expert, lo, hi, mode = schedule
    rows_in = pl.BlockSpec((tm, d), lambda v, tile, *_: (jnp.minimum(tile[v], last_in), 0))
    rows_out = pl.BlockSpec((tm, d), lambda v, tile, *_: (tile[v], 0))
    wspec = pl.BlockSpec((1, d, d), lambda v, tile, expert, *_: (expert[v], 0, 0))
    bspec = pl.BlockSpec((1, 1, d), lambda v, tile, expert, *_: (expert[v], 0, 0))
    return pl.pallas_call(
        _expert_kernel,
        grid_spec=pltpu.PrefetchScalarGridSpec(
            num_scalar_prefetch=5,
            grid=(tile.shape[0],),
            in_specs=[rows_in, wspec, bspec, wspec, bspec, wspec, bspec],
            out_specs=rows_out),
        out_shape=jax.ShapeDtypeStruct((n_rows + tm, d), BF16),
        compiler_params=_params("arbitrary"),
        name="experts",
    )(tile, expert, lo, hi, mode, xs, wg, bg, wu, bu, wd, bd)


def _combine_kernel(tiles_p, n_p, t_p, b_p, t_s, win_ref, npass_ref, ys_hbm, idx_ref, loc_ref,
                    prob_ref, x1p_ref, x1s_ref, mod_ref, gf_ref, yp_ref, ysm_ref, win_buf, sem):
    i = pl.program_id(0)
    n = pl.num_programs(0)
    tm = idx_ref.shape[0]
    slot = i % 2
    w_rows = COMBINE_WINDOW

    def fetch(step, p, s):
        for e in range(N_EXPERTS):
            start = pl.multiple_of(win_ref[step * N_EXPERTS + e] + p * w_rows, BF16_SUBLANES)
            pltpu.make_async_copy(ys_hbm.at[pl.ds(start, w_rows)],
                                  win_buf.at[s, pl.ds(e * w_rows, w_rows)], sem.at[s]).start()

    def wait(s):
        pltpu.make_async_copy(ys_hbm.at[pl.ds(0, N_EXPERTS * w_rows)], win_buf.at[s],
                              sem.at[s]).wait()

    idx = idx_ref[...]
    loc = loc_ref[...]
    prob = prob_ref[...]
    lane = lax.broadcasted_iota(jnp.int32, (tm, N_EXPERTS * w_rows), 1)

    def mix(p, s):
        coef = jnp.zeros(lane.shape, F32)
        for k in range(TOP_K):
            lw = loc[:, k:k + 1] - p * w_rows
            col = jnp.where((lw >= 0) & (lw < w_rows), idx[:, k:k + 1] * w_rows + lw, -1)
            coef = jnp.where(lane == col, prob[:, k:k + 1], coef)
        return jnp.dot(coef.astype(BF16), win_buf[s], preferred_element_type=F32)

    @pl.when(i == 0)
    def _():
        fetch(0, 0, 0)

    @pl.when(i + 1 < n)
    def _():
        fetch(i + 1, 0, 1 - slot)

    wait(slot)
    moe = mix(0, slot)

    def extra_pass(p, acc):
        fetch(i, p, slot)
        wait(slot)
        return acc + mix(p, slot)

    moe = lax.fori_loop(1, npass_ref[i], extra_pass, moe)
    x1 = jnp.where(i < tiles_p, x1p_ref[...], x1s_ref[...])
    gate_rows = []
    for j in range(tm // CHUNK):
        n0 = i * tm + j * CHUNK
        b = jnp.where(n0 < n_p, n0 // t_p, b_p + (n0 - n_p) // t_s)
        gate_rows.append(jnp.broadcast_to(mod_ref[b, 5:6, :], (CHUNK, D_MODEL)))
    x2 = x1 + jnp.concatenate(gate_rows, axis=0) * moe
    y = _rms(x2) * gf_ref[...]

    @pl.when(i < tiles_p)
    def _():
        yp_ref[...] = y

    @pl.when(i >= tiles_p)
    def _():
        ysm_ref[...] = y


def _combine(ys, win_start, npass, idx, loc, prob, x1_p, x1_s, mod, t_p, t_s, g_final):
    (n_p, d), n_s = x1_p.shape, x1_s.shape[0]
    tm = TOK_TILE
    n = n_p + n_s
    assert t_p % CHUNK == 0 and t_s % CHUNK == 0 and tm % CHUNK == 0
    tiles_p = n_p // tm
    last_p = tiles_p - 1
    spec_p = pl.BlockSpec((tm, d), lambda i, *_: (jnp.minimum(i, last_p), 0))
    spec_s = pl.BlockSpec((tm, d), lambda i, *_: (jnp.maximum(i - tiles_p, 0), 0))
    tok4 = pl.BlockSpec((tm, TOP_K), lambda i, *_: (i, 0))
    return pl.pallas_call(
        functools.partial(_combine_kernel, tiles_p, n_p, t_p, n_p // t_p, t_s),
        grid_spec=pltpu.PrefetchScalarGridSpec(
            num_scalar_prefetch=2,
            grid=(n // tm,),
            in_specs=[pl.BlockSpec(memory_space=pl.ANY), tok4, tok4, tok4, spec_p, spec_s,
                      pl.BlockSpec(mod.shape, lambda i, *_: (0, 0, 0)),
                      pl.BlockSpec((1, d), lambda i, *_: (0, 0))],
            out_specs=[spec_p, spec_s],
            scratch_shapes=[pltpu.VMEM((2, N_EXPERTS * COMBINE_WINDOW, d), BF16),
                            pltpu.SemaphoreType.DMA((2,))]),
        out_shape=[jax.ShapeDtypeStruct((n_p, d), F32), jax.ShapeDtypeStruct((n_s, d), F32)],
        compiler_params=_params("arbitrary"),
        name="combine",
    )(win_start, npass, ys, idx, loc, prob, x1_p, x1_s, mod, g_final)


def _rope_tables(t, pos0):
    half = RET_QK_DIM // 2
    inv = ROPE_BASE ** (-jnp.arange(0, RET_QK_DIM, 2, dtype=F32) / RET_QK_DIM)
    pos = jnp.arange(t, dtype=F32) + float(pos0)
    ang = pos[:, None] * inv[None, :]
    assert ang.shape == (t, half)
    return jnp.cos(ang), jnp.sin(ang)


def _mixer(x, mod, pos0, conv_cache, s0, wts):
    bsz, t, d = x.shape
    cos, sin = _rope_tables(t, pos0)
    q, k, v, gs, cb, u, sga, sgb = _in_proj(x, mod, wts["g1"], cos, sin, wts["w_in"])
    yc, conv_new = _conv(u, cb, conv_cache, wts["conv_w"], wts["w_conv_out"])
    x1, xn2, s_new = _ret(q, k, v, gs, yc, sga, sgb, x, mod, s0,
                          wts["w_ret_out"], wts["w_mix_out"], wts["g2"])
    return x1.reshape(bsz * t, d), xn2.reshape(bsz * t, d), conv_new[None], s_new[None]


def _moe(x1_p, xn2_p, x1_s, xn2_s, mod, t_p, t_s, wts):
    n = x1_p.shape[0] + x1_s.shape[0]
    idx, prob, rank, tile_base, counts = _router(xn2_p, xn2_s, wts["w_router"], wts["b_router"])
    counts = counts.reshape(N_EXPERTS)
    *schedule, off = _visit_schedule(counts, n * TOP_K, EXPERT_ROWS)
    experts = jnp.arange(N_EXPERTS, dtype=jnp.int32)
    pick = lambda table: jnp.sum(jnp.where(idx[..., None] == experts, table, 0), axis=-1)
    pos = pick(off[:-1]) + rank
    xs = _dispatch(xn2_p, xn2_s, pos.reshape(n * TOP_K))
    ys = _experts(xs, schedule, wts["w_gate"], wts["b_gate"], wts["w_up"], wts["b_up"],
                  wts["w_down"], wts["b_down"])
    seg_start = off[:-1][None, :] + tile_base.reshape(-1, N_EXPERTS)
    win_start = (seg_start // BF16_SUBLANES) * BF16_SUBLANES
    loc = pos - pick(jnp.repeat(win_start, TOK_TILE, axis=0)[:, None, :])
    npass = jnp.max(loc.reshape(-1, TOK_TILE * TOP_K), axis=1) // COMBINE_WINDOW + 1
    return _combine(ys, win_start.reshape(-1), npass, idx, loc, prob, x1_p, x1_s, mod, t_p, t_s,
                    wts["g_final"])


def kernel(x_prompt, x_sample, c_prompt, c_sample, cache_conv, state_ret, w_ada, b_ada, g_norm1,
           w_in, conv_w, w_ret_out, w_conv_out, w_mix_out, g_norm2, w_router, b_router,
           w_gate, b_gate, w_up, b_up, w_down, b_down, g_final):
    assert w_ada.shape[0] == 1, "single-layer model"
    d = D_MODEL
    bp = x_prompt.shape[0]
    bs = x_sample.shape[0]
    wts = {
        "g1": g_norm1[0].reshape(1, d),
        "w_in": w_in[0].astype(BF16),
        "conv_w": conv_w[0],
        "w_ret_out": w_ret_out[0].astype(BF16),
        "w_conv_out": w_conv_out[0].astype(BF16),
        "w_mix_out": w_mix_out[0].astype(BF16),
        "g2": g_norm2[0].reshape(1, d),
        "w_router": w_router[0].astype(BF16),
        "b_router": b_router[0],
        "w_gate": w_gate[0].astype(BF16),
        "b_gate": b_gate[0].reshape(N_EXPERTS, 1, d),
        "w_up": w_up[0].astype(BF16),
        "b_up": b_up[0].reshape(N_EXPERTS, 1, d),
        "w_down": w_down[0].astype(BF16),
        "b_down": b_down[0].reshape(N_EXPERTS, 1, d),
        "g_final": g_final.reshape(1, d),
    }
    mod = _ada(jnp.concatenate([c_prompt, c_sample], axis=0), w_ada[0], b_ada[0])
    mod = mod.reshape(bp + bs, 6, d)
    conv0 = jnp.zeros((bp, CONV_K - 1, d), F32)
    ret0 = jnp.zeros((bp, RET_HEADS, RET_QK_DIM, RET_V_DIM), F32)
    t_p, t_s = x_prompt.shape[1], x_sample.shape[1]
    x1_p, xn2_p, conv_p, ret_p = _mixer(x_prompt, mod[:bp], 0, conv0, ret0, wts)
    x1_s, xn2_s, conv_s, ret_s = _mixer(x_sample, mod[bp:], PAST_LEN, cache_conv[0], state_ret[0],
                                        wts)
    y_p, y_s = _moe(x1_p, xn2_p, x1_s, xn2_s, mod, t_p, t_s, wts)
    return (y_p.reshape(x_prompt.shape), y_s.reshape(x_sample.shape), conv_p, ret_p, conv_s, ret_s)
```

```python
import functools

import jax
import jax.numpy as jnp
import numpy as np
from jax import lax
from jax.experimental import pallas as pl
from jax.experimental.pallas import tpu as pltpu

D_MODEL = 1024
CHUNK = 64
RET_HEADS = 4
RET_QK_DIM = 256
RET_V_DIM = 512
RET_QK = RET_HEADS * RET_QK_DIM
RET_V = RET_HEADS * RET_V_DIM
CONV_K = 3
N_EXPERTS = 32
TOP_K = 4
SWIGLU_LIMIT = 7.0
SWIGLU_ALPHA = 1.702
ROPE_BASE = 10000.0
PAST_LEN = 4096
EPS = 1e-6
PROJ_WIDTH = 2 * RET_QK + 2 * RET_V + 5 * D_MODEL

V7X_VMEM_LIMIT_BYTES = 56 * 1024 * 1024
RET_BLOCK = 256
PROJ_ROWS = 256
TOK_TILE = 256
EXPERT_ROWS = 512
COMBINE_WINDOW = 64
BF16_SUBLANES = 16
F32_SUBLANES = 8
VISIT_FIRST, VISIT_MERGE, VISIT_ZERO = 1, 0, 2

F32 = jnp.float32
BF16 = jnp.bfloat16


def _params(*sem):
    return pltpu.CompilerParams(dimension_semantics=sem, vmem_limit_bytes=V7X_VMEM_LIMIT_BYTES)


def _sigmoid(x):
    return 1.0 / (1.0 + jnp.exp(-x))


def _rms(x):
    return x * lax.rsqrt(jnp.mean(x * x, axis=-1, keepdims=True) + EPS)


def _ada_kernel(c_ref, w_ref, b_ref, o_ref):
    c = c_ref[...]
    s = c * _sigmoid(c)
    o_ref[...] = jnp.dot(s, w_ref[...], preferred_element_type=F32,
                         precision=lax.Precision.HIGHEST) + b_ref[...]


def _ada(c, w, b):
    n, d = c.shape
    width = w.shape[1]
    tn = 1024
    return pl.pallas_call(
        _ada_kernel,
        grid=(width // tn,),
        in_specs=[pl.BlockSpec((n, d), lambda j: (0, 0)),
                  pl.BlockSpec((d, tn), lambda j: (0, j)),
                  pl.BlockSpec((1, tn), lambda j: (0, j))],
        out_specs=pl.BlockSpec((n, tn), lambda j: (0, j)),
        out_shape=jax.ShapeDtypeStruct((n, width), F32),
        compiler_params=_params("arbitrary"),
        name="ada",
    )(c, w, b.reshape(1, width))


def _in_proj_kernel(x_ref, mod_ref, g1_ref, cos_ref, sin_ref, w_ref,
                    q_ref, k_ref, v_ref, gs_ref, cb_ref, u_ref, sga_ref, sgb_ref):
    x = x_ref[0]
    mod = mod_ref[0]
    xn = _rms(x) * g1_ref[...]
    xn = (xn * (1.0 + mod[1:2]) + mod[0:1]).astype(BF16)
    cos = cos_ref[...]
    sin = sin_ref[...]

    def proj(c0, width):
        return jnp.dot(xn, w_ref[:, c0:c0 + width], preferred_element_type=F32)

    def rope_store(acc, out_ref, scale):
        half = RET_QK_DIM // 2
        for h in range(RET_HEADS):
            a = acc[:, h * RET_QK_DIM: h * RET_QK_DIM + half]
            b = acc[:, h * RET_QK_DIM + half: (h + 1) * RET_QK_DIM]
            out_ref[0, :, h * RET_QK_DIM: h * RET_QK_DIM + half] = (
                (a * cos - b * sin) * scale).astype(BF16)
            out_ref[0, :, h * RET_QK_DIM + half: (h + 1) * RET_QK_DIM] = (
                (a * sin + b * cos) * scale).astype(BF16)

    c0 = 0
    rope_store(proj(c0, RET_QK), q_ref, 1.0)
    c0 += RET_QK
    rope_store(proj(c0, RET_QK), k_ref, RET_QK_DIM ** -0.5)
    c0 += RET_QK
    for j in range(RET_V // 1024):
        v_ref[0, :, j * 1024:(j + 1) * 1024] = proj(c0 + j * 1024, 1024).astype(BF16)
    c0 += RET_V
    for j in range(RET_V // 1024):
        g = proj(c0 + j * 1024, 1024)
        gs_ref[0, :, j * 1024:(j + 1) * 1024] = (g * _sigmoid(g)).astype(BF16)
    c0 += RET_V
    cb_ref[0] = proj(c0, D_MODEL).astype(BF16)
    c0 += D_MODEL
    cc = proj(c0, D_MODEL)
    c0 += D_MODEL
    u_ref[0] = cc * proj(c0, D_MODEL)
    c0 += D_MODEL
    sga_ref[0] = _sigmoid(proj(c0, D_MODEL)).astype(BF16)
    c0 += D_MODEL
    sgb_ref[0] = _sigmoid(proj(c0, D_MODEL)).astype(BF16)


def _in_proj(x, mod, g1, cos, sin, w_in_bf16):
    bsz, t, d = x.shape
    tm = min(PROJ_ROWS, t)
    row = lambda width: pl.BlockSpec((1, tm, width), lambda b, i: (b, i, 0))
    shp = lambda width, dt: jax.ShapeDtypeStruct((bsz, t, width), dt)
    return pl.pallas_call(
        _in_proj_kernel,
        grid=(bsz, t // tm),
        in_specs=[row(d),
                  pl.BlockSpec((1, 6, d), lambda b, i: (b, 0, 0)),
                  pl.BlockSpec((1, d), lambda b, i: (0, 0)),
                  pl.BlockSpec((tm, RET_QK_DIM // 2), lambda b, i: (i, 0)),
                  pl.BlockSpec((tm, RET_QK_DIM // 2), lambda b, i: (i, 0)),
                  pl.BlockSpec((d, PROJ_WIDTH), lambda b, i: (0, 0),
                               pipeline_mode=pl.Buffered(1))],
        out_specs=[row(RET_QK), row(RET_QK), row(RET_V), row(RET_V),
                   row(d), row(d), row(d), row(d)],
        out_shape=[shp(RET_QK, BF16), shp(RET_QK, BF16), shp(RET_V, BF16), shp(RET_V, BF16),
                   shp(d, BF16), shp(d, F32), shp(d, BF16), shp(d, BF16)],
        compiler_params=_params("parallel", "parallel"),
        name="in_proj",
    )(x, mod, g1, cos, sin, w_in_bf16)


def _conv_kernel(u_ref, cb_ref, cache_ref, cw_ref, w_ref, y_ref, new_ref, prev_ref):
    i = pl.program_id(1)

    @pl.when(i == 0)
    def _():
        prev_ref[...] = cache_ref[0]

    u = u_ref[0]
    tc = u.shape[0]
    prev = prev_ref[...]
    rows = lax.broadcasted_iota(jnp.int32, u.shape, 0)
    u1 = jnp.where(rows == 0, prev[1:2], pltpu.roll(u, 1, axis=0))
    u2 = jnp.where(rows == 0, prev[0:1], jnp.where(rows == 1, prev[1:2], pltpu.roll(u, 2, axis=0)))
    cw = cw_ref[...]
    conv = u2 * cw[0:1] + u1 * cw[1:2] + u * cw[2:3]
    z = (cb_ref[0].astype(F32) * conv).astype(BF16)
    y_ref[0] = jnp.dot(z, w_ref[...], preferred_element_type=F32).astype(BF16)
    tail = u_ref[0, tc - 2:tc, :]
    prev_ref[...] = tail
    new_ref[0] = tail


def _conv(u, cb, cache, conv_w, w_conv_out_bf16):
    bsz, t, d = u.shape
    tc = min(512, t)
    return pl.pallas_call(
        _conv_kernel,
        grid=(bsz, t // tc),
        in_specs=[pl.BlockSpec((1, tc, d), lambda b, i: (b, i, 0)),
                  pl.BlockSpec((1, tc, d), lambda b, i: (b, i, 0)),
                  pl.BlockSpec((1, CONV_K - 1, d), lambda b, i: (b, 0, 0)),
                  pl.BlockSpec((CONV_K, d), lambda b, i: (0, 0)),
                  pl.BlockSpec((d, d), lambda b, i: (0, 0))],
        out_specs=[pl.BlockSpec((1, tc, d), lambda b, i: (b, i, 0)),
                   pl.BlockSpec((1, CONV_K - 1, d), lambda b, i: (b, 0, 0))],
        out_shape=[jax.ShapeDtypeStruct((bsz, t, d), BF16),
                   jax.ShapeDtypeStruct((bsz, CONV_K - 1, d), F32)],
        scratch_shapes=[pltpu.VMEM((CONV_K - 1, d), F32)],
        compiler_params=_params("parallel", "arbitrary"),
        name="conv",
    )(u, cb, cache, conv_w, w_conv_out_bf16)


def _decay_tables(block):
    lg = np.log(1.0 - np.exp2(-5.0 - np.arange(RET_HEADS, dtype=np.float64)))
    idx = np.arange(block, dtype=np.float64)
    dist = np.abs(idx[:, None] - idx[None, :])
    visible = (idx[None, :] // CHUNK) <= (idx[:, None] // CHUNK)
    dmat = np.where(visible[None], np.exp(dist[None] * lg[:, None, None]), 0.0)
    qdec = np.exp((idx + 1.0)[None, :] * lg[:, None])
    kdec = np.exp((block - 1.0 - idx)[None, :] * lg[:, None])
    sdec = np.exp(block * lg)
    lanes = lambda a: jnp.asarray(np.broadcast_to(a[:, :, None], (RET_HEADS, block, 128)), F32)
    return jnp.asarray(dmat, F32), lanes(qdec), lanes(kdec), [float(s) for s in sdec]


def _ret_kernel(sdec, q_ref, k_ref, v_ref, gs_ref, yc_ref, sga_ref, sgb_ref, x_ref, mod_ref,
                s0_ref, dmat_ref, qdec_ref, kdec_ref, wro_ref, wmo_ref, g2_ref,
                x1_ref, xn2_ref, s_ref):
    i = pl.program_id(1)

    @pl.when(i == 0)
    def _():
        s_ref[...] = s0_ref[...]

    ls = q_ref.shape[1]
    yret = jnp.zeros((ls, D_MODEL), F32)
    for h in range(RET_HEADS):
        qh = q_ref[0, :, h * RET_QK_DIM:(h + 1) * RET_QK_DIM]
        kh = k_ref[0, :, h * RET_QK_DIM:(h + 1) * RET_QK_DIM]
        vh = v_ref[0, :, h * RET_V_DIM:(h + 1) * RET_V_DIM]
        state = s_ref[0, h]
        scores = lax.dot_general(qh, kh, (((1,), (1,)), ((), ())), preferred_element_type=F32)
        p = (scores * dmat_ref[h]).astype(BF16)
        o = jnp.dot(p, vh, preferred_element_type=F32)
        qdec = jnp.concatenate([qdec_ref[h]] * (RET_V_DIM // 128), axis=1)
        o = o + jnp.dot(qh, state.astype(BF16), preferred_element_type=F32) * qdec
        kdec = jnp.concatenate([kdec_ref[h]] * (RET_QK_DIM // 128), axis=1)
        kd = (kh.astype(F32) * kdec).astype(BF16)
        s_ref[0, h] = sdec[h] * state + lax.dot_general(
            kd, vh, (((0,), (0,)), ((), ())), preferred_element_type=F32)
        og = (gs_ref[0, :, h * RET_V_DIM:(h + 1) * RET_V_DIM].astype(F32) * _rms(o)).astype(BF16)
        yret = yret + jnp.dot(og, wro_ref[h * RET_V_DIM:(h + 1) * RET_V_DIM, :],
                              preferred_element_type=F32)
    hmix = (sga_ref[0].astype(F32) * yc_ref[0].astype(F32) + sgb_ref[0].astype(F32) * yret)
    mix = jnp.dot(hmix.astype(BF16), wmo_ref[...], preferred_element_type=F32)
    mod = mod_ref[0]
    x1 = x_ref[0] + mod[2:3] * mix
    x1_ref[0] = x1
    xn2 = _rms(x1) * g2_ref[...]
    xn2_ref[0] = xn2 * (1.0 + mod[4:5]) + mod[3:4]


def _ret(q, k, v, gs, yc, sga, sgb, x, mod, s0, w_ret_out_bf16, w_mix_out_bf16, g2):
    bsz, t, d = x.shape
    ls = min(RET_BLOCK, t)
    dmat, qdec, kdec, sdec = _decay_tables(ls)
    row = lambda width: pl.BlockSpec((1, ls, width), lambda b, i: (b, i, 0))
    const = lambda shape: pl.BlockSpec(shape, lambda b, i: (0,) * len(shape))
    state_spec = pl.BlockSpec((1, RET_HEADS, RET_QK_DIM, RET_V_DIM), lambda b, i: (b, 0, 0, 0))
    return pl.pallas_call(
        functools.partial(_ret_kernel, sdec),
        grid=(bsz, t // ls),
        in_specs=[row(RET_QK), row(RET_QK), row(RET_V), row(RET_V), row(d), row(d), row(d), row(d),
                  pl.BlockSpec((1, 6, d), lambda b, i: (b, 0, 0)),
                  state_spec,
                  const((RET_HEADS, ls, ls)), const((RET_HEADS, ls, 128)),
                  const((RET_HEADS, ls, 128)),
                  const((RET_V, d)), const((d, d)), const((1, d))],
        out_specs=[row(d), row(d), state_spec],
        out_shape=[jax.ShapeDtypeStruct((bsz, t, d), F32),
                   jax.ShapeDtypeStruct((bsz, t, d), F32),
                   jax.ShapeDtypeStruct(s0.shape, F32)],
        compiler_params=_params("parallel", "arbitrary"),
        name="ret",
    )(q, k, v, gs, yc, sga, sgb, x, mod, s0, dmat, qdec, kdec, w_ret_out_bf16, w_mix_out_bf16, g2)


def _two_stream_specs(n_p, n_s, tm, width):
    tiles_p = n_p // tm
    last_p = tiles_p - 1
    return (pl.BlockSpec((tm, width), lambda i: (jnp.minimum(i, last_p), 0)),
            pl.BlockSpec((tm, width), lambda i: (jnp.maximum(i - tiles_p, 0), 0)))


def _router_kernel(tiles_p, xp_ref, xs_ref, w_ref, b_ref, idx_ref, prob_ref, rank_ref, base_ref,
                   cnt_ref, run_ref):
    i = pl.program_id(0)

    @pl.when(i == 0)
    def _():
        run_ref[...] = jnp.zeros_like(run_ref)

    base_ref[0] = run_ref[...].astype(jnp.int32)

    xn = jnp.where(i < tiles_p, xp_ref[...], xs_ref[...]).astype(BF16)
    logits = jnp.dot(xn, w_ref[...], preferred_element_type=F32) + b_ref[...]
    tm = logits.shape[0]
    lane = lax.broadcasted_iota(jnp.int32, logits.shape, 1)
    col = lax.broadcasted_iota(jnp.int32, (tm, TOP_K), 1)
    work = logits
    sels, vals = [], []
    idx = jnp.zeros((tm, TOP_K), jnp.int32)
    for k in range(TOP_K):
        m = jnp.max(work, axis=-1, keepdims=True)
        first = jnp.min(jnp.where(work == m, lane, N_EXPERTS), axis=-1, keepdims=True)
        sel = lane == first
        sels.append(sel)
        vals.append(m)
        idx = jnp.where(col == k, first, idx)
        work = jnp.where(sel, -jnp.inf, work)
    exps = [jnp.exp(v - vals[0]) for v in vals]
    denom = exps[0] + exps[1] + exps[2] + exps[3]
    prob = jnp.zeros((tm, TOP_K), F32)
    for k in range(TOP_K):
        prob = jnp.where(col == k, exps[k] / denom, prob)
    member = jnp.zeros(logits.shape, F32)
    for sel in sels:
        member = member + jnp.where(sel, 1.0, 0.0)
    r = lax.broadcasted_iota(jnp.int32, (tm, tm), 0)
    c = lax.broadcasted_iota(jnp.int32, (tm, tm), 1)
    earlier = jnp.where(r > c, 1.0, 0.0).astype(BF16)
    before = jnp.dot(earlier, member.astype(BF16), preferred_element_type=F32) + run_ref[...]
    rank = jnp.zeros((tm, TOP_K), jnp.int32)
    for k in range(TOP_K):
        rk = jnp.sum(jnp.where(sels[k], before, 0.0), axis=-1, keepdims=True)
        rank = jnp.where(col == k, rk.astype(jnp.int32), rank)
    run_ref[...] += jnp.sum(member, axis=0, keepdims=True)
    idx_ref[...] = idx
    prob_ref[...] = prob
    rank_ref[...] = rank
    cnt_ref[...] = run_ref[...].astype(jnp.int32)


def _router(xn2_p, xn2_s, w_router_bf16, b_router):
    (n_p, d), n_s = xn2_p.shape, xn2_s.shape[0]
    tm = TOK_TILE
    n = n_p + n_s
    spec_p, spec_s = _two_stream_specs(n_p, n_s, tm, d)
    tok4 = pl.BlockSpec((tm, TOP_K), lambda i: (i, 0))
    return pl.pallas_call(
        functools.partial(_router_kernel, n_p // tm),
        grid=(n // tm,),
        in_specs=[spec_p, spec_s,
                  pl.BlockSpec((d, N_EXPERTS), lambda i: (0, 0)),
                  pl.BlockSpec((1, N_EXPERTS), lambda i: (0, 0))],
        out_specs=[tok4, tok4, tok4,
                   pl.BlockSpec((1, 1, N_EXPERTS), lambda i: (i, 0, 0)),
                   pl.BlockSpec((1, N_EXPERTS), lambda i: (0, 0))],
        out_shape=[jax.ShapeDtypeStruct((n, TOP_K), jnp.int32),
                   jax.ShapeDtypeStruct((n, TOP_K), F32),
                   jax.ShapeDtypeStruct((n, TOP_K), jnp.int32),
                   jax.ShapeDtypeStruct((n // tm, 1, N_EXPERTS), jnp.int32),
                   jax.ShapeDtypeStruct((1, N_EXPERTS), jnp.int32)],
        scratch_shapes=[pltpu.VMEM((1, N_EXPERTS), F32)],
        compiler_params=_params("arbitrary"),
        name="router",
    )(xn2_p, xn2_s, w_router_bf16, b_router.reshape(1, N_EXPERTS))


def _dispatch_kernel(tiles_p, pos_hbm, xp_ref, xs_ref, out_hbm, pos_smem, row_buf, sem_pos, sem_rows):
    i = pl.program_id(0)
    n = pl.num_programs(0)
    groups, sub, d = row_buf.shape[1:]
    tm = groups * sub
    slot = i % 2

    def pos_copy(step, s):
        return pltpu.make_async_copy(pos_hbm.at[pl.ds(step * (tm * TOP_K), tm * TOP_K)],
                                     pos_smem.at[pl.ds(s * (tm * TOP_K), tm * TOP_K)],
                                     sem_pos.at[s])

    def drain(s):
        for _ in range(TOP_K):
            pltpu.make_async_copy(xp_ref, out_hbm.at[pl.ds(0, tm)], sem_rows.at[s]).wait()

    @pl.when(i == 0)
    def _():
        pos_copy(0, 0).start()

    @pl.when(i + 1 < n)
    def _():
        pos_copy(i + 1, 1 - slot).start()

    @pl.when(i >= 2)
    def _():
        drain(slot)

    row_buf[slot] = jnp.where(i < tiles_p, xp_ref[...], xs_ref[...]).reshape(groups, sub, d)
    pos_copy(i, slot).wait()
    pos_base = slot * (tm * TOP_K)

    def issue(g, carry):
        for u in range(sub):
            for k in range(TOP_K):
                dst = pos_smem[pos_base + g * (sub * TOP_K) + (u * TOP_K + k)]
                pltpu.make_async_copy(row_buf.at[slot, g, pl.ds(u, 1)], out_hbm.at[pl.ds(dst, 1)],
                                      sem_rows.at[slot]).start(priority=k % 2)
        return carry

    lax.fori_loop(0, groups, issue, 0)

    @pl.when(i == n - 1)
    def _():
        drain(slot)

    @pl.when((i == n - 1) & (n >= 2))
    def _():
        drain(1 - slot)


def _dispatch(xn2_p, xn2_s, pos_flat):
    (n_p, d), n_s = xn2_p.shape, xn2_s.shape[0]
    tm = TOK_TILE
    n = n_p + n_s
    spec_p, spec_s = _two_stream_specs(n_p, n_s, tm, d)
    return pl.pallas_call(
        functools.partial(_dispatch_kernel, n_p // tm),
        grid=(n // tm,),
        in_specs=[pl.BlockSpec(memory_space=pl.ANY), spec_p, spec_s],
        out_specs=pl.BlockSpec(memory_space=pl.ANY),
        out_shape=jax.ShapeDtypeStruct((n * TOP_K, d), F32),
        scratch_shapes=[pltpu.SMEM((2 * tm * TOP_K,), jnp.int32),
                        pltpu.VMEM((2, tm // F32_SUBLANES, F32_SUBLANES, d), F32),
                        pltpu.SemaphoreType.DMA((2,)),
                        pltpu.SemaphoreType.DMA((2,))],
        compiler_params=_params("arbitrary"),
        name="dispatch",
    )(pos_flat, xn2_p, xn2_s)


def _visit_schedule(counts, n_rows, tm):
    n_tiles = n_rows // tm
    n_visits = n_tiles + N_EXPERTS
    off = jnp.concatenate([jnp.zeros((1,), jnp.int32), jnp.cumsum(counts, dtype=jnp.int32)])
    first_tile = off[:-1] // tm
    last_tile = (off[1:] - 1) // tm
    per_expert = jnp.where(counts > 0, last_tile - first_tile + 1, 0)
    ends = jnp.cumsum(per_expert, dtype=jnp.int32)
    total = ends[-1]
    step = jnp.arange(n_visits, dtype=jnp.int32)
    v = jnp.minimum(step, total - 1)
    expert = jnp.sum((ends[None, :] <= v[:, None]).astype(jnp.int32), axis=1)
    mine = expert[:, None] == jnp.arange(N_EXPERTS, dtype=jnp.int32)[None, :]
    pick = lambda table: jnp.sum(jnp.where(mine, table[None, :], 0), axis=1)
    live = step < total
    tile = jnp.where(live, pick(first_tile) + (v - pick(ends - per_expert)), n_tiles)
    lo = jnp.where(live, jnp.clip(pick(off[:-1]) - tile * tm, 0, tm), 0)
    hi = jnp.where(live, jnp.clip(pick(off[1:]) - tile * tm, 0, tm), 0)
    changed = jnp.concatenate([jnp.ones((1,), jnp.bool_), tile[1:] != tile[:-1]])
    mode = jnp.where(live, jnp.where(changed, VISIT_FIRST, VISIT_MERGE),
                     jnp.where(step == total, VISIT_ZERO, VISIT_MERGE)).astype(jnp.int32)
    return tile, expert, lo, hi, mode, off


def _expert_kernel(tile_ref, expert_ref, lo_ref, hi_ref, mode_ref,
                   x_ref, wg_ref, bg_ref, wu_ref, bu_ref, wd_ref, bd_ref, y_ref):
    v = pl.program_id(0)
    lo = lo_ref[v]
    hi = hi_ref[v]

    @pl.when(mode_ref[v] == VISIT_ZERO)
    def _():
        y_ref[...] = jnp.zeros_like(y_ref)

    @pl.when(hi > lo)
    def _():
        xn = x_ref[...].astype(BF16)
        gl = jnp.minimum(jnp.dot(xn, wg_ref[0], preferred_element_type=F32) + bg_ref[0],
                         SWIGLU_LIMIT)
        lin = jnp.clip(jnp.dot(xn, wu_ref[0], preferred_element_type=F32) + bu_ref[0],
                       -SWIGLU_LIMIT, SWIGLU_LIMIT)
        act = gl * _sigmoid(SWIGLU_ALPHA * gl) * (lin + 1.0)
        out = jnp.dot(act.astype(BF16), wd_ref[0], preferred_element_type=F32) + bd_ref[0]
        row = lax.broadcasted_iota(jnp.int32, out.shape, 0)
        mine = (row >= lo) & (row < hi)

        @pl.when(mode_ref[v] == VISIT_FIRST)
        def _():
            y_ref[...] = pltpu.bitcast(jnp.where(mine, out, 0.0).astype(BF16), jnp.uint32)

        @pl.when(mode_ref[v] == VISIT_MERGE)
        def _():
            prev = pltpu.bitcast(y_ref[...], BF16)
            y_ref[...] = pltpu.bitcast(jnp.where(mine, out.astype(BF16), prev), jnp.uint32)


def _experts(xs, schedule, wg, bg, wu, bu, wd, bd):
    n_rows, d = xs.shape
    tm = EXPERT_ROWS
    last_in = n_rows // tm - 1
    tile, expert, lo, hi, mode = schedule
    rows_in = pl.BlockSpec((tm, d), lambda v, tile, *_: (jnp.minimum(tile[v], last_in), 0))
    rows_out = pl.BlockSpec((tm // 2, d), lambda v, tile, *_: (tile[v], 0))
    wspec = pl.BlockSpec((1, d, d), lambda v, tile, expert, *_: (expert[v], 0, 0))
    bspec = pl.BlockSpec((1, 1, d), lambda v, tile, expert, *_: (expert[v], 0, 0))
    return pl.pallas_call(
        _expert_kernel,
        grid_spec=pltpu.PrefetchScalarGridSpec(
            num_scalar_prefetch=5,
            grid=(tile.shape[0],),
            in_specs=[rows_in, wspec, bspec, wspec, bspec, wspec, bspec],
            out_specs=rows_out),
        out_shape=jax.ShapeDtypeStruct(((n_rows + tm) // 2, d), jnp.uint32),
        compiler_params=_params("arbitrary"),
        name="experts",
    )(tile, expert, lo, hi, mode, xs, wg, bg, wu, bu, wd, bd)


def _combine_kernel(tiles_p, n_p, t_p, b_p, t_s, win_ref, npass_ref, ys_hbm, idx_ref, loc_ref,
                    prob_ref, x1p_ref, x1s_ref, mod_ref, gf_ref, yp_ref, ysm_ref, win_buf, sem):
    i = pl.program_id(0)
    n = pl.num_programs(0)
    tm = idx_ref.shape[0]
    slot = i % 2
    w_rows = COMBINE_WINDOW
    w_words = w_rows // 2

    def fetch(step, p, s):
        for e in range(N_EXPERTS):
            start = pl.multiple_of(win_ref[step * N_EXPERTS + e] + p * w_words, F32_SUBLANES)
            pltpu.make_async_copy(ys_hbm.at[pl.ds(start, w_words)],
                                  win_buf.at[s, pl.ds(e * w_words, w_words)],
                                  sem.at[s]).start(priority=e % 2)

    def wait(s):
        pltpu.make_async_copy(ys_hbm.at[pl.ds(0, N_EXPERTS * w_words)], win_buf.at[s],
                              sem.at[s]).wait()

    idx = idx_ref[...]
    loc = loc_ref[...]
    prob = prob_ref[...]
    lane = lax.broadcasted_iota(jnp.int32, (tm, N_EXPERTS * w_rows), 1)

    def mix(p, s):
        coef = jnp.zeros(lane.shape, F32)
        for k in range(TOP_K):
            lw = loc[:, k:k + 1] - p * w_rows
            col = jnp.where((lw >= 0) & (lw < w_rows), idx[:, k:k + 1] * w_rows + lw, -1)
            coef = jnp.where(lane == col, prob[:, k:k + 1], coef)
        rows = pltpu.bitcast(win_buf[s], BF16)
        return jnp.dot(coef.astype(BF16), rows, preferred_element_type=F32)

    @pl.when(i == 0)
    def _():
        fetch(0, 0, 0)

    @pl.when(i + 1 < n)
    def _():
        fetch(i + 1, 0, 1 - slot)

    wait(slot)
    moe = mix(0, slot)

    def extra_pass(p, acc):
        fetch(i, p, slot)
        wait(slot)
        return acc + mix(p, slot)

    moe = lax.fori_loop(1, npass_ref[i], extra_pass, moe)
    x1 = jnp.where(i < tiles_p, x1p_ref[...], x1s_ref[...])
    gate_rows = []
    for j in range(tm // CHUNK):
        n0 = i * tm + j * CHUNK
        b = jnp.where(n0 < n_p, n0 // t_p, b_p + (n0 - n_p) // t_s)
        gate_rows.append(jnp.broadcast_to(mod_ref[b, 5:6, :], (CHUNK, D_MODEL)))
    x2 = x1 + jnp.concatenate(gate_rows, axis=0) * moe
    y = _rms(x2) * gf_ref[...]

    @pl.when(i < tiles_p)
    def _():
        yp_ref[...] = y

    @pl.when(i >= tiles_p)
    def _():
        ysm_ref[...] = y


def _combine(ys, win_start, npass, idx, loc, prob, x1_p, x1_s, mod, t_p, t_s, g_final):
    (n_p, d), n_s = x1_p.shape, x1_s.shape[0]
    tm = TOK_TILE
    n = n_p + n_s
    assert t_p % CHUNK == 0 and t_s % CHUNK == 0 and tm % CHUNK == 0
    tiles_p = n_p // tm
    last_p = tiles_p - 1
    spec_p = pl.BlockSpec((tm, d), lambda i, *_: (jnp.minimum(i, last_p), 0))
    spec_s = pl.BlockSpec((tm, d), lambda i, *_: (jnp.maximum(i - tiles_p, 0), 0))
    tok4 = pl.BlockSpec((tm, TOP_K), lambda i, *_: (i, 0))
    return pl.pallas_call(
        functools.partial(_combine_kernel, tiles_p, n_p, t_p, n_p // t_p, t_s),
        grid_spec=pltpu.PrefetchScalarGridSpec(
            num_scalar_prefetch=2,
            grid=(n // tm,),
            in_specs=[pl.BlockSpec(memory_space=pl.ANY), tok4, tok4, tok4, spec_p, spec_s,
                      pl.BlockSpec(mod.shape, lambda i, *_: (0, 0, 0)),
                      pl.BlockSpec((1, d), lambda i, *_: (0, 0))],
            out_specs=[spec_p, spec_s],
            scratch_shapes=[pltpu.VMEM((2, N_EXPERTS * COMBINE_WINDOW // 2, d), jnp.uint32),
                            pltpu.SemaphoreType.DMA((2,))]),
        out_shape=[jax.ShapeDtypeStruct((n_p, d), F32), jax.ShapeDtypeStruct((n_s, d), F32)],
        compiler_params=_params("arbitrary"),
        name="combine",
    )(win_start, npass, ys, idx, loc, prob, x1_p, x1_s, mod, g_final)


def _rope_tables(t, pos0):
    half = RET_QK_DIM // 2
    inv = ROPE_BASE ** (-jnp.arange(0, RET_QK_DIM, 2, dtype=F32) / RET_QK_DIM)
    pos = jnp.arange(t, dtype=F32) + float(pos0)
    ang = pos[:, None] * inv[None, :]
    assert ang.shape == (t, half)
    return jnp.cos(ang), jnp.sin(ang)


def _mixer(x, mod, pos0, conv_cache, s0, wts):
    bsz, t, d = x.shape
    cos, sin = _rope_tables(t, pos0)
    q, k, v, gs, cb, u, sga, sgb = _in_proj(x, mod, wts["g1"], cos, sin, wts["w_in"])
    yc, conv_new = _conv(u, cb, conv_cache, wts["conv_w"], wts["w_conv_out"])
    x1, xn2, s_new = _ret(q, k, v, gs, yc, sga, sgb, x, mod, s0,
                          wts["w_ret_out"], wts["w_mix_out"], wts["g2"])
    return x1.reshape(bsz * t, d), xn2.reshape(bsz * t, d), conv_new[None], s_new[None]


def _moe(x1_p, xn2_p, x1_s, xn2_s, mod, t_p, t_s, wts):
    n = x1_p.shape[0] + x1_s.shape[0]
    idx, prob, rank, tile_base, counts = _router(xn2_p, xn2_s, wts["w_router"], wts["b_router"])
    counts = counts.reshape(N_EXPERTS)
    *schedule, off = _visit_schedule(counts, n * TOP_K, EXPERT_ROWS)
    experts = jnp.arange(N_EXPERTS, dtype=jnp.int32)
    pick = lambda table: jnp.sum(jnp.where(idx[..., None] == experts, table, 0), axis=-1)
    pos = pick(off[:-1]) + rank
    xs = _dispatch(xn2_p, xn2_s, pos.reshape(n * TOP_K))
    ys = _experts(xs, schedule, wts["w_gate"], wts["b_gate"], wts["w_up"], wts["b_up"],
                  wts["w_down"], wts["b_down"])
    seg_start = off[:-1][None, :] + tile_base.reshape(-1, N_EXPERTS)
    win_start = (seg_start // BF16_SUBLANES) * BF16_SUBLANES
    loc = pos - pick(jnp.repeat(win_start, TOK_TILE, axis=0)[:, None, :])
    npass = jnp.max(loc.reshape(-1, TOK_TILE * TOP_K), axis=1) // COMBINE_WINDOW + 1
    return _combine(ys, (win_start // 2).reshape(-1), npass, idx, loc, prob,
                    x1_p, x1_s, mod, t_p, t_s, wts["g_final"])


def kernel(x_prompt, x_sample, c_prompt, c_sample, cache_conv, state_ret, w_ada, b_ada, g_norm1,
           w_in, conv_w, w_ret_out, w_conv_out, w_mix_out, g_norm2, w_router, b_router,
           w_gate, b_gate, w_up, b_up, w_down, b_down, g_final):
    assert w_ada.shape[0] == 1, "single-layer model"
    d = D_MODEL
    bp = x_prompt.shape[0]
    bs = x_sample.shape[0]
    wts = {
        "g1": g_norm1[0].reshape(1, d),
        "w_in": w_in[0].astype(BF16),
        "conv_w": conv_w[0],
        "w_ret_out": w_ret_out[0].astype(BF16),
        "w_conv_out": w_conv_out[0].astype(BF16),
        "w_mix_out": w_mix_out[0].astype(BF16),
        "g2": g_norm2[0].reshape(1, d),
        "w_router": w_router[0].astype(BF16),
        "b_router": b_router[0],
        "w_gate": w_gate[0].astype(BF16),
        "b_gate": b_gate[0].reshape(N_EXPERTS, 1, d),
        "w_up": w_up[0].astype(BF16),
        "b_up": b_up[0].reshape(N_EXPERTS, 1, d),
        "w_down": w_down[0].astype(BF16),
        "b_down": b_down[0].reshape(N_EXPERTS, 1, d),
        "g_final": g_final.reshape(1, d),
    }
    mod = _ada(jnp.concatenate([c_prompt, c_sample], axis=0), w_ada[0], b_ada[0])
    mod = mod.reshape(bp + bs, 6, d)
    conv0 = jnp.zeros((bp, CONV_K - 1, d), F32)
    ret0 = jnp.zeros((bp, RET_HEADS, RET_QK_DIM, RET_V_DIM), F32)
    t_p, t_s = x_prompt.shape[1], x_sample.shape[1]
    x1_p, xn2_p, conv_p, ret_p = _mixer(x_prompt, mod[:bp], 0, conv0, ret0, wts)
    x1_s, xn2_s, conv_s, ret_s = _mixer(x_sample, mod[bp:], PAST_LEN, cache_conv[0], state_ret[0],
                                        wts)
    y_p, y_s = _moe(x1_p, xn2_p, x1_s, xn2_s, mod, t_p, t_s, wts)
    return (y_p.reshape(x_prompt.shape), y_s.reshape(x_sample.shape), conv_p, ret_p, conv_s, ret_s)
```

```python
import functools

import jax
import jax.numpy as jnp
import numpy as np
from jax import lax
from jax.experimental import pallas as pl
from jax.experimental.pallas import tpu as pltpu

D_MODEL = 1024
CHUNK = 64
RET_HEADS = 4
RET_QK_DIM = 256
RET_V_DIM = 512
RET_QK = RET_HEADS * RET_QK_DIM
RET_V = RET_HEADS * RET_V_DIM
CONV_K = 3
N_EXPERTS = 32
TOP_K = 4
SWIGLU_LIMIT = 7.0
SWIGLU_ALPHA = 1.702
ROPE_BASE = 10000.0
PAST_LEN = 4096
EPS = 1e-6
PROJ_WIDTH = 2 * RET_QK + 2 * RET_V + 5 * D_MODEL

V7X_VMEM_LIMIT_BYTES = 56 * 1024 * 1024
RET_BLOCK = 256
PROJ_ROWS = 256
TOK_TILE = 256
EXPERT_ROWS = 512
COMBINE_WINDOW = 64
COMBINE_SLOTS = 40
BF16_SUBLANES = 16
F32_SUBLANES = 8
VISIT_FIRST, VISIT_MERGE, VISIT_ZERO = 1, 0, 2

F32 = jnp.float32
BF16 = jnp.bfloat16


def _params(*sem):
    return pltpu.CompilerParams(dimension_semantics=sem, vmem_limit_bytes=V7X_VMEM_LIMIT_BYTES)


def _sigmoid(x):
    return 1.0 / (1.0 + jnp.exp(-x))


def _rms(x):
    return x * lax.rsqrt(jnp.mean(x * x, axis=-1, keepdims=True) + EPS)


def _ada_kernel(c_ref, w_ref, b_ref, o_ref):
    c = c_ref[...]
    s = c * _sigmoid(c)
    o_ref[...] = jnp.dot(s, w_ref[...], preferred_element_type=F32,
                         precision=lax.Precision.HIGHEST) + b_ref[...]


def _ada(c, w, b):
    n, d = c.shape
    width = w.shape[1]
    tn = 1024
    return pl.pallas_call(
        _ada_kernel,
        grid=(width // tn,),
        in_specs=[pl.BlockSpec((n, d), lambda j: (0, 0)),
                  pl.BlockSpec((d, tn), lambda j: (0, j)),
                  pl.BlockSpec((1, tn), lambda j: (0, j))],
        out_specs=pl.BlockSpec((n, tn), lambda j: (0, j)),
        out_shape=jax.ShapeDtypeStruct((n, width), F32),
        compiler_params=_params("arbitrary"),
        name="ada",
    )(c, w, b.reshape(1, width))


def _in_proj_kernel(x_ref, mod_ref, g1_ref, cos_ref, sin_ref, w_ref,
                    q_ref, k_ref, v_ref, gs_ref, cb_ref, u_ref, sga_ref, sgb_ref):
    x = x_ref[0]
    mod = mod_ref[0]
    xn = _rms(x) * g1_ref[...]
    xn = (xn * (1.0 + mod[1:2]) + mod[0:1]).astype(BF16)
    cos = cos_ref[...]
    sin = sin_ref[...]

    def proj(c0, width):
        return jnp.dot(xn, w_ref[:, c0:c0 + width], preferred_element_type=F32)

    def rope_store(acc, out_ref, scale):
        half = RET_QK_DIM // 2
        for h in range(RET_HEADS):
            a = acc[:, h * RET_QK_DIM: h * RET_QK_DIM + half]
            b = acc[:, h * RET_QK_DIM + half: (h + 1) * RET_QK_DIM]
            out_ref[0, :, h * RET_QK_DIM: h * RET_QK_DIM + half] = (
                (a * cos - b * sin) * scale).astype(BF16)
            out_ref[0, :, h * RET_QK_DIM + half: (h + 1) * RET_QK_DIM] = (
                (a * sin + b * cos) * scale).astype(BF16)

    c0 = 0
    rope_store(proj(c0, RET_QK), q_ref, 1.0)
    c0 += RET_QK
    rope_store(proj(c0, RET_QK), k_ref, RET_QK_DIM ** -0.5)
    c0 += RET_QK
    for j in range(RET_V // 1024):
        v_ref[0, :, j * 1024:(j + 1) * 1024] = proj(c0 + j * 1024, 1024).astype(BF16)
    c0 += RET_V
    for j in range(RET_V // 1024):
        g = proj(c0 + j * 1024, 1024)
        gs_ref[0, :, j * 1024:(j + 1) * 1024] = (g * _sigmoid(g)).astype(BF16)
    c0 += RET_V
    cb_ref[0] = proj(c0, D_MODEL).astype(BF16)
    c0 += D_MODEL
    cc = proj(c0, D_MODEL)
    c0 += D_MODEL
    u_ref[0] = cc * proj(c0, D_MODEL)
    c0 += D_MODEL
    sga_ref[0] = _sigmoid(proj(c0, D_MODEL)).astype(BF16)
    c0 += D_MODEL
    sgb_ref[0] = _sigmoid(proj(c0, D_MODEL)).astype(BF16)


def _in_proj(x, mod, g1, cos, sin, w_in_bf16):
    bsz, t, d = x.shape
    tm = min(PROJ_ROWS, t)
    row = lambda width: pl.BlockSpec((1, tm, width), lambda b, i: (b, i, 0))
    shp = lambda width, dt: jax.ShapeDtypeStruct((bsz, t, width), dt)
    return pl.pallas_call(
        _in_proj_kernel,
        grid=(bsz, t // tm),
        in_specs=[row(d),
                  pl.BlockSpec((1, 6, d), lambda b, i: (b, 0, 0)),
                  pl.BlockSpec((1, d), lambda b, i: (0, 0)),
                  pl.BlockSpec((tm, RET_QK_DIM // 2), lambda b, i: (i, 0)),
                  pl.BlockSpec((tm, RET_QK_DIM // 2), lambda b, i: (i, 0)),
                  pl.BlockSpec((d, PROJ_WIDTH), lambda b, i: (0, 0),
                               pipeline_mode=pl.Buffered(1))],
        out_specs=[row(RET_QK), row(RET_QK), row(RET_V), row(RET_V),
                   row(d), row(d), row(d), row(d)],
        out_shape=[shp(RET_QK, BF16), shp(RET_QK, BF16), shp(RET_V, BF16), shp(RET_V, BF16),
                   shp(d, BF16), shp(d, F32), shp(d, BF16), shp(d, BF16)],
        compiler_params=_params("parallel", "parallel"),
        name="in_proj",
    )(x, mod, g1, cos, sin, w_in_bf16)


def _conv_kernel(u_ref, cb_ref, cache_ref, cw_ref, w_ref, y_ref, new_ref, prev_ref):
    i = pl.program_id(1)

    @pl.when(i == 0)
    def _():
        prev_ref[...] = cache_ref[0]

    u = u_ref[0]
    tc = u.shape[0]
    prev = prev_ref[...]
    rows = lax.broadcasted_iota(jnp.int32, u.shape, 0)
    u1 = jnp.where(rows == 0, prev[1:2], pltpu.roll(u, 1, axis=0))
    u2 = jnp.where(rows == 0, prev[0:1], jnp.where(rows == 1, prev[1:2], pltpu.roll(u, 2, axis=0)))
    cw = cw_ref[...]
    conv = u2 * cw[0:1] + u1 * cw[1:2] + u * cw[2:3]
    z = (cb_ref[0].astype(F32) * conv).astype(BF16)
    y_ref[0] = jnp.dot(z, w_ref[...], preferred_element_type=F32).astype(BF16)
    tail = u_ref[0, tc - 2:tc, :]
    prev_ref[...] = tail
    new_ref[0] = tail


def _conv(u, cb, cache, conv_w, w_conv_out_bf16):
    bsz, t, d = u.shape
    tc = min(512, t)
    return pl.pallas_call(
        _conv_kernel,
        grid=(bsz, t // tc),
        in_specs=[pl.BlockSpec((1, tc, d), lambda b, i: (b, i, 0)),
                  pl.BlockSpec((1, tc, d), lambda b, i: (b, i, 0)),
                  pl.BlockSpec((1, CONV_K - 1, d), lambda b, i: (b, 0, 0)),
                  pl.BlockSpec((CONV_K, d), lambda b, i: (0, 0)),
                  pl.BlockSpec((d, d), lambda b, i: (0, 0))],
        out_specs=[pl.BlockSpec((1, tc, d), lambda b, i: (b, i, 0)),
                   pl.BlockSpec((1, CONV_K - 1, d), lambda b, i: (b, 0, 0))],
        out_shape=[jax.ShapeDtypeStruct((bsz, t, d), BF16),
                   jax.ShapeDtypeStruct((bsz, CONV_K - 1, d), F32)],
        scratch_shapes=[pltpu.VMEM((CONV_K - 1, d), F32)],
        compiler_params=_params("parallel", "arbitrary"),
        name="conv",
    )(u, cb, cache, conv_w, w_conv_out_bf16)


def _decay_tables(block):
    lg = np.log(1.0 - np.exp2(-5.0 - np.arange(RET_HEADS, dtype=np.float64)))
    idx = np.arange(block, dtype=np.float64)
    dist = np.abs(idx[:, None] - idx[None, :])
    visible = (idx[None, :] // CHUNK) <= (idx[:, None] // CHUNK)
    dmat = np.where(visible[None], np.exp(dist[None] * lg[:, None, None]), 0.0)
    qdec = np.exp((idx + 1.0)[None, :] * lg[:, None])
    kdec = np.exp((block - 1.0 - idx)[None, :] * lg[:, None])
    sdec = np.exp(block * lg)
    lanes = lambda a: jnp.asarray(np.broadcast_to(a[:, :, None], (RET_HEADS, block, 128)), F32)
    return jnp.asarray(dmat, F32), lanes(qdec), lanes(kdec), [float(s) for s in sdec]


def _ret_kernel(sdec, q_ref, k_ref, v_ref, gs_ref, yc_ref, sga_ref, sgb_ref, x_ref, mod_ref,
                s0_ref, dmat_ref, qdec_ref, kdec_ref, wro_ref, wmo_ref, g2_ref,
                x1_ref, xn2_ref, s_ref):
    i = pl.program_id(1)

    @pl.when(i == 0)
    def _():
        s_ref[...] = s0_ref[...]

    ls = q_ref.shape[1]
    yret = jnp.zeros((ls, D_MODEL), F32)
    for h in range(RET_HEADS):
        qh = q_ref[0, :, h * RET_QK_DIM:(h + 1) * RET_QK_DIM]
        kh = k_ref[0, :, h * RET_QK_DIM:(h + 1) * RET_QK_DIM]
        vh = v_ref[0, :, h * RET_V_DIM:(h + 1) * RET_V_DIM]
        state = s_ref[0, h]
        scores = lax.dot_general(qh, kh, (((1,), (1,)), ((), ())), preferred_element_type=F32)
        p = (scores * dmat_ref[h]).astype(BF16)
        o = jnp.dot(p, vh, preferred_element_type=F32)
        qdec = jnp.concatenate([qdec_ref[h]] * (RET_V_DIM // 128), axis=1)
        o = o + jnp.dot(qh, state.astype(BF16), preferred_element_type=F32) * qdec
        kdec = jnp.concatenate([kdec_ref[h]] * (RET_QK_DIM // 128), axis=1)
        kd = (kh.astype(F32) * kdec).astype(BF16)
        s_ref[0, h] = sdec[h] * state + lax.dot_general(
            kd, vh, (((0,), (0,)), ((), ())), preferred_element_type=F32)
        og = (gs_ref[0, :, h * RET_V_DIM:(h + 1) * RET_V_DIM].astype(F32) * _rms(o)).astype(BF16)
        yret = yret + jnp.dot(og, wro_ref[h * RET_V_DIM:(h + 1) * RET_V_DIM, :],
                              preferred_element_type=F32)
    hmix = (sga_ref[0].astype(F32) * yc_ref[0].astype(F32) + sgb_ref[0].astype(F32) * yret)
    mix = jnp.dot(hmix.astype(BF16), wmo_ref[...], preferred_element_type=F32)
    mod = mod_ref[0]
    x1 = x_ref[0] + mod[2:3] * mix
    x1_ref[0] = x1
    xn2 = _rms(x1) * g2_ref[...]
    xn2_ref[0] = xn2 * (1.0 + mod[4:5]) + mod[3:4]


def _ret(q, k, v, gs, yc, sga, sgb, x, mod, s0, w_ret_out_bf16, w_mix_out_bf16, g2):
    bsz, t, d = x.shape
    ls = min(RET_BLOCK, t)
    dmat, qdec, kdec, sdec = _decay_tables(ls)
    row = lambda width: pl.BlockSpec((1, ls, width), lambda b, i: (b, i, 0))
    const = lambda shape: pl.BlockSpec(shape, lambda b, i: (0,) * len(shape))
    state_spec = pl.BlockSpec((1, RET_HEADS, RET_QK_DIM, RET_V_DIM), lambda b, i: (b, 0, 0, 0))
    return pl.pallas_call(
        functools.partial(_ret_kernel, sdec),
        grid=(bsz, t // ls),
        in_specs=[row(RET_QK), row(RET_QK), row(RET_V), row(RET_V), row(d), row(d), row(d), row(d),
                  pl.BlockSpec((1, 6, d), lambda b, i: (b, 0, 0)),
                  state_spec,
                  const((RET_HEADS, ls, ls)), const((RET_HEADS, ls, 128)),
                  const((RET_HEADS, ls, 128)),
                  const((RET_V, d)), const((d, d)), const((1, d))],
        out_specs=[row(d), row(d), state_spec],
        out_shape=[jax.ShapeDtypeStruct((bsz, t, d), F32),
                   jax.ShapeDtypeStruct((bsz, t, d), F32),
                   jax.ShapeDtypeStruct(s0.shape, F32)],
        compiler_params=_params("parallel", "arbitrary"),
        name="ret",
    )(q, k, v, gs, yc, sga, sgb, x, mod, s0, dmat, qdec, kdec, w_ret_out_bf16, w_mix_out_bf16, g2)


def _two_stream_specs(n_p, n_s, tm, width):
    tiles_p = n_p // tm
    last_p = tiles_p - 1
    return (pl.BlockSpec((tm, width), lambda i: (jnp.minimum(i, last_p), 0)),
            pl.BlockSpec((tm, width), lambda i: (jnp.maximum(i - tiles_p, 0), 0)))


def _router_kernel(tiles_p, xp_ref, xs_ref, w_ref, b_ref, idx_ref, prob_ref, rank_ref, base_ref,
                   cnt_ref, run_ref):
    i = pl.program_id(0)

    @pl.when(i == 0)
    def _():
        run_ref[...] = jnp.zeros_like(run_ref)

    base_ref[0] = run_ref[...].astype(jnp.int32)

    xn = jnp.where(i < tiles_p, xp_ref[...], xs_ref[...]).astype(BF16)
    logits = jnp.dot(xn, w_ref[...], preferred_element_type=F32) + b_ref[...]
    tm = logits.shape[0]
    lane = lax.broadcasted_iota(jnp.int32, logits.shape, 1)
    col = lax.broadcasted_iota(jnp.int32, (tm, TOP_K), 1)
    work = logits
    sels, vals = [], []
    idx = jnp.zeros((tm, TOP_K), jnp.int32)
    for k in range(TOP_K):
        m = jnp.max(work, axis=-1, keepdims=True)
        first = jnp.min(jnp.where(work == m, lane, N_EXPERTS), axis=-1, keepdims=True)
        sel = lane == first
        sels.append(sel)
        vals.append(m)
        idx = jnp.where(col == k, first, idx)
        work = jnp.where(sel, -jnp.inf, work)
    exps = [jnp.exp(v - vals[0]) for v in vals]
    denom = exps[0] + exps[1] + exps[2] + exps[3]
    prob = jnp.zeros((tm, TOP_K), F32)
    for k in range(TOP_K):
        prob = jnp.where(col == k, exps[k] / denom, prob)
    member = jnp.zeros(logits.shape, F32)
    for sel in sels:
        member = member + jnp.where(sel, 1.0, 0.0)
    r = lax.broadcasted_iota(jnp.int32, (tm, tm), 0)
    c = lax.broadcasted_iota(jnp.int32, (tm, tm), 1)
    earlier = jnp.where(r > c, 1.0, 0.0).astype(BF16)
    before = jnp.dot(earlier, member.astype(BF16), preferred_element_type=F32) + run_ref[...]
    rank = jnp.zeros((tm, TOP_K), jnp.int32)
    for k in range(TOP_K):
        rk = jnp.sum(jnp.where(sels[k], before, 0.0), axis=-1, keepdims=True)
        rank = jnp.where(col == k, rk.astype(jnp.int32), rank)
    run_ref[...] += jnp.sum(member, axis=0, keepdims=True)
    idx_ref[...] = idx
    prob_ref[...] = prob
    rank_ref[...] = rank
    cnt_ref[...] = run_ref[...].astype(jnp.int32)


def _router(xn2_p, xn2_s, w_router_bf16, b_router):
    (n_p, d), n_s = xn2_p.shape, xn2_s.shape[0]
    tm = TOK_TILE
    n = n_p + n_s
    spec_p, spec_s = _two_stream_specs(n_p, n_s, tm, d)
    tok4 = pl.BlockSpec((tm, TOP_K), lambda i: (i, 0))
    return pl.pallas_call(
        functools.partial(_router_kernel, n_p // tm),
        grid=(n // tm,),
        in_specs=[spec_p, spec_s,
                  pl.BlockSpec((d, N_EXPERTS), lambda i: (0, 0)),
                  pl.BlockSpec((1, N_EXPERTS), lambda i: (0, 0))],
        out_specs=[tok4, tok4, tok4,
                   pl.BlockSpec((1, 1, N_EXPERTS), lambda i: (i, 0, 0)),
                   pl.BlockSpec((1, N_EXPERTS), lambda i: (0, 0))],
        out_shape=[jax.ShapeDtypeStruct((n, TOP_K), jnp.int32),
                   jax.ShapeDtypeStruct((n, TOP_K), F32),
                   jax.ShapeDtypeStruct((n, TOP_K), jnp.int32),
                   jax.ShapeDtypeStruct((n // tm, 1, N_EXPERTS), jnp.int32),
                   jax.ShapeDtypeStruct((1, N_EXPERTS), jnp.int32)],
        scratch_shapes=[pltpu.VMEM((1, N_EXPERTS), F32)],
        compiler_params=_params("arbitrary"),
        name="router",
    )(xn2_p, xn2_s, w_router_bf16, b_router.reshape(1, N_EXPERTS))


def _dispatch_kernel(tiles_p, pos_hbm, xp_ref, xs_ref, out_hbm, pos_smem, row_buf, sem_pos, sem_rows):
    i = pl.program_id(0)
    n = pl.num_programs(0)
    groups, sub, d = row_buf.shape[1:]
    tm = groups * sub
    slot = i % 2

    def pos_copy(step, s):
        return pltpu.make_async_copy(pos_hbm.at[pl.ds(step * (tm * TOP_K), tm * TOP_K)],
                                     pos_smem.at[pl.ds(s * (tm * TOP_K), tm * TOP_K)],
                                     sem_pos.at[s])

    def drain(s):
        for _ in range(TOP_K):
            pltpu.make_async_copy(xp_ref, out_hbm.at[pl.ds(0, tm)], sem_rows.at[s]).wait()

    @pl.when(i == 0)
    def _():
        pos_copy(0, 0).start()

    @pl.when(i + 1 < n)
    def _():
        pos_copy(i + 1, 1 - slot).start()

    @pl.when(i >= 2)
    def _():
        drain(slot)

    row_buf[slot] = jnp.where(i < tiles_p, xp_ref[...], xs_ref[...]).reshape(groups, sub, d)
    pos_copy(i, slot).wait()
    pos_base = slot * (tm * TOP_K)

    def issue(g, carry):
        for u in range(sub):
            for k in range(TOP_K):
                dst = pos_smem[pos_base + g * (sub * TOP_K) + (u * TOP_K + k)]
                pltpu.make_async_copy(row_buf.at[slot, g, pl.ds(u, 1)], out_hbm.at[pl.ds(dst, 1)],
                                      sem_rows.at[slot]).start(priority=k % 2)
        return carry

    lax.fori_loop(0, groups, issue, 0)

    @pl.when(i == n - 1)
    def _():
        drain(slot)

    @pl.when((i == n - 1) & (n >= 2))
    def _():
        drain(1 - slot)


def _dispatch(xn2_p, xn2_s, pos_flat):
    (n_p, d), n_s = xn2_p.shape, xn2_s.shape[0]
    tm = TOK_TILE
    n = n_p + n_s
    spec_p, spec_s = _two_stream_specs(n_p, n_s, tm, d)
    return pl.pallas_call(
        functools.partial(_dispatch_kernel, n_p // tm),
        grid=(n // tm,),
        in_specs=[pl.BlockSpec(memory_space=pl.ANY), spec_p, spec_s],
        out_specs=pl.BlockSpec(memory_space=pl.ANY),
        out_shape=jax.ShapeDtypeStruct((n * TOP_K, d), F32),
        scratch_shapes=[pltpu.SMEM((2 * tm * TOP_K,), jnp.int32),
                        pltpu.VMEM((2, tm // F32_SUBLANES, F32_SUBLANES, d), F32),
                        pltpu.SemaphoreType.DMA((2,)),
                        pltpu.SemaphoreType.DMA((2,))],
        compiler_params=_params("arbitrary"),
        name="dispatch",
    )(pos_flat, xn2_p, xn2_s)


def _visit_schedule(counts, n_rows, tm):
    n_tiles = n_rows // tm
    n_visits = n_tiles + N_EXPERTS
    off = jnp.concatenate([jnp.zeros((1,), jnp.int32), jnp.cumsum(counts, dtype=jnp.int32)])
    first_tile = off[:-1] // tm
    last_tile = (off[1:] - 1) // tm
    per_expert = jnp.where(counts > 0, last_tile - first_tile + 1, 0)
    ends = jnp.cumsum(per_expert, dtype=jnp.int32)
    total = ends[-1]
    step = jnp.arange(n_visits, dtype=jnp.int32)
    v = jnp.minimum(step, total - 1)
    expert = jnp.sum((ends[None, :] <= v[:, None]).astype(jnp.int32), axis=1)
    mine = expert[:, None] == jnp.arange(N_EXPERTS, dtype=jnp.int32)[None, :]
    pick = lambda table: jnp.sum(jnp.where(mine, table[None, :], 0), axis=1)
    live = step < total
    tile = jnp.where(live, pick(first_tile) + (v - pick(ends - per_expert)), n_tiles)
    lo = jnp.where(live, jnp.clip(pick(off[:-1]) - tile * tm, 0, tm), 0)
    hi = jnp.where(live, jnp.clip(pick(off[1:]) - tile * tm, 0, tm), 0)
    changed = jnp.concatenate([jnp.ones((1,), jnp.bool_), tile[1:] != tile[:-1]])
    mode = jnp.where(live, jnp.where(changed, VISIT_FIRST, VISIT_MERGE),
                     jnp.where(step == total, VISIT_ZERO, VISIT_MERGE)).astype(jnp.int32)
    return tile, expert, lo, hi, mode, off


def _expert_kernel(tile_ref, expert_ref, lo_ref, hi_ref, mode_ref,
                   x_ref, wg_ref, bg_ref, wu_ref, bu_ref, wd_ref, bd_ref, y_ref, w_bf16):
    v = pl.program_id(0)
    lo = lo_ref[v]
    hi = hi_ref[v]

    @pl.when((v == 0) | (expert_ref[v] != expert_ref[jnp.maximum(v - 1, 0)]))
    def _():
        w_bf16[0] = wg_ref[0].astype(BF16)
        w_bf16[1] = wu_ref[0].astype(BF16)
        w_bf16[2] = wd_ref[0].astype(BF16)

    @pl.when(mode_ref[v] == VISIT_ZERO)
    def _():
        y_ref[...] = jnp.zeros_like(y_ref)

    @pl.when(hi > lo)
    def _():
        xn = x_ref[...].astype(BF16)
        gl = jnp.minimum(jnp.dot(xn, w_bf16[0], preferred_element_type=F32) + bg_ref[0],
                         SWIGLU_LIMIT)
        lin = jnp.clip(jnp.dot(xn, w_bf16[1], preferred_element_type=F32) + bu_ref[0],
                       -SWIGLU_LIMIT, SWIGLU_LIMIT)
        act = gl * _sigmoid(SWIGLU_ALPHA * gl) * (lin + 1.0)
        out = jnp.dot(act.astype(BF16), w_bf16[2], preferred_element_type=F32) + bd_ref[0]
        row = lax.broadcasted_iota(jnp.int32, out.shape, 0)
        mine = (row >= lo) & (row < hi)

        @pl.when(mode_ref[v] == VISIT_FIRST)
        def _():
            y_ref[...] = pltpu.bitcast(jnp.where(mine, out, 0.0).astype(BF16), jnp.uint32)

        @pl.when(mode_ref[v] == VISIT_MERGE)
        def _():
            prev = pltpu.bitcast(y_ref[...], BF16)
            y_ref[...] = pltpu.bitcast(jnp.where(mine, out.astype(BF16), prev), jnp.uint32)


def _experts(xs, schedule, wg, bg, wu, bu, wd, bd):
    n_rows, d = xs.shape
    tm = EXPERT_ROWS
    last_in = n_rows // tm - 1
    tile, expert, lo, hi, mode = schedule
    rows_in = pl.BlockSpec((tm, d), lambda v, tile, *_: (jnp.minimum(tile[v], last_in), 0))
    rows_out = pl.BlockSpec((tm // 2, d), lambda v, tile, *_: (tile[v], 0))
    wspec = pl.BlockSpec((1, d, d), lambda v, tile, expert, *_: (expert[v], 0, 0))
    bspec = pl.BlockSpec((1, 1, d), lambda v, tile, expert, *_: (expert[v], 0, 0))
    return pl.pallas_call(
        _expert_kernel,
        grid_spec=pltpu.PrefetchScalarGridSpec(
            num_scalar_prefetch=5,
            grid=(tile.shape[0],),
            in_specs=[rows_in, wspec, bspec, wspec, bspec, wspec, bspec],
            out_specs=rows_out,
            scratch_shapes=[pltpu.VMEM((3, d, d), BF16)]),
        out_shape=jax.ShapeDtypeStruct(((n_rows + tm) // 2, d), jnp.uint32),
        compiler_params=_params("arbitrary"),
        name="experts",
    )(tile, expert, lo, hi, mode, xs, wg, bg, wu, bu, wd, bd)


def _combine_plan(pos, idx, off, counts, tile_base, n_rows):
    w, slots = COMBINE_WINDOW, COMBINE_SLOTS
    base = tile_base.reshape(-1, N_EXPERTS)
    n_tiles = base.shape[0]
    seg_start = off[:-1][None, :] + base
    n_seg = jnp.concatenate([base[1:], counts[None, :]], axis=0) - base
    win_start = (seg_start // BF16_SUBLANES) * BF16_SUBLANES
    n_win = jnp.where(n_seg > 0, (seg_start - win_start + n_seg + w - 1) // w, 0)
    slot_end = jnp.cumsum(n_win, axis=1, dtype=jnp.int32)
    slot_base = slot_end - n_win
    experts = jnp.arange(N_EXPERTS, dtype=jnp.int32)
    per_token = lambda table: jnp.repeat(table, TOK_TILE, axis=0)[:, None, :]
    pick = lambda table: jnp.sum(jnp.where(idx[..., None] == experts, per_token(table), 0), axis=-1)
    loc = pos - pick(win_start)
    slot = pick(slot_base) + loc // w
    col = jnp.where(slot < slots, slot * w + loc % w, -1)
    q = jnp.arange(slots, dtype=jnp.int32)[None, :, None]
    e_q = jnp.sum((slot_end[:, None, :] <= q).astype(jnp.int32), axis=-1)
    of_slot = lambda table: jnp.sum(jnp.where(e_q[..., None] == experts, table[:, None, :], 0), -1)
    used = q[..., 0] < slot_end[:, -1:]
    slot_start = jnp.where(used, of_slot(win_start) + (q[..., 0] - of_slot(slot_base)) * w, n_rows)
    fb_pass = jnp.where(col < 0, loc // w + 1, 0).reshape(n_tiles, -1)
    return col, slot_start, win_start, loc, jnp.max(fb_pass, axis=1)


def _combine_kernel(tiles_p, n_p, t_p, b_p, t_s, slot_ref, fb_ref, npass_ref, ys_hbm, col_ref,
                    idx_ref, loc_ref, prob_ref, x1p_ref, x1s_ref, mod_ref, gf_ref, yp_ref, ysm_ref,
                    win_buf, sem):
    i = pl.program_id(0)
    n = pl.num_programs(0)
    tm = idx_ref.shape[0]
    slot = i % 2
    w_rows = COMBINE_WINDOW
    w_words = w_rows // 2

    def fetch(starts_ref, count, first, shift, s):
        for j in range(count):
            start = pl.multiple_of(starts_ref[first + j] + shift, F32_SUBLANES)
            pltpu.make_async_copy(ys_hbm.at[pl.ds(start, w_words)],
                                  win_buf.at[s, pl.ds(j * w_words, w_words)],
                                  sem.at[s]).start(priority=j % 2)

    def wait(count, s):
        pltpu.make_async_copy(ys_hbm.at[pl.ds(0, count * w_words)],
                              win_buf.at[s, pl.ds(0, count * w_words)], sem.at[s]).wait()

    def mix(cols, count, s):
        lane = lax.broadcasted_iota(jnp.int32, (tm, count * w_rows), 1)
        prob = prob_ref[...]
        coef = jnp.zeros(lane.shape, F32)
        for k in range(TOP_K):
            coef = jnp.where(lane == cols[:, k:k + 1], prob[:, k:k + 1], coef)
        rows = pltpu.bitcast(win_buf[s, 0:count * w_words, :], BF16)
        return jnp.dot(coef.astype(BF16), rows, preferred_element_type=F32)

    @pl.when(i == 0)
    def _():
        fetch(slot_ref, COMBINE_SLOTS, 0, 0, 0)

    @pl.when(i + 1 < n)
    def _():
        fetch(slot_ref, COMBINE_SLOTS, (i + 1) * COMBINE_SLOTS, 0, 1 - slot)

    wait(COMBINE_SLOTS, slot)
    covered = col_ref[...]
    moe = mix(covered, COMBINE_SLOTS, slot)

    def fallback_pass(p, acc):
        fetch(fb_ref, N_EXPERTS, i * N_EXPERTS, p * w_words, slot)
        wait(N_EXPERTS, slot)
        lw = loc_ref[...] - p * w_rows
        cols = jnp.where((covered < 0) & (lw >= 0) & (lw < w_rows), idx_ref[...] * w_rows + lw, -1)
        return acc + mix(cols, N_EXPERTS, slot)

    moe = lax.fori_loop(0, npass_ref[i], fallback_pass, moe)
    x1 = jnp.where(i < tiles_p, x1p_ref[...], x1s_ref[...])
    gate_rows = []
    for j in range(tm // CHUNK):
        n0 = i * tm + j * CHUNK
        b = jnp.where(n0 < n_p, n0 // t_p, b_p + (n0 - n_p) // t_s)
        gate_rows.append(jnp.broadcast_to(mod_ref[b, 5:6, :], (CHUNK, D_MODEL)))
    x2 = x1 + jnp.concatenate(gate_rows, axis=0) * moe
    y = _rms(x2) * gf_ref[...]

    @pl.when(i < tiles_p)
    def _():
        yp_ref[...] = y

    @pl.when(i >= tiles_p)
    def _():
        ysm_ref[...] = y


def _combine(ys, plan, idx, prob, x1_p, x1_s, mod, t_p, t_s, g_final):
    col, slot_start, fb_start, loc, fb_passes = plan
    (n_p, d), n_s = x1_p.shape, x1_s.shape[0]
    tm = TOK_TILE
    n = n_p + n_s
    assert t_p % CHUNK == 0 and t_s % CHUNK == 0 and tm % CHUNK == 0
    tiles_p = n_p // tm
    last_p = tiles_p - 1
    spec_p = pl.BlockSpec((tm, d), lambda i, *_: (jnp.minimum(i, last_p), 0))
    spec_s = pl.BlockSpec((tm, d), lambda i, *_: (jnp.maximum(i - tiles_p, 0), 0))
    tok4 = pl.BlockSpec((tm, TOP_K), lambda i, *_: (i, 0))
    return pl.pallas_call(
        functools.partial(_combine_kernel, tiles_p, n_p, t_p, n_p // t_p, t_s),
        grid_spec=pltpu.PrefetchScalarGridSpec(
            num_scalar_prefetch=3,
            grid=(n // tm,),
            in_specs=[pl.BlockSpec(memory_space=pl.ANY), tok4, tok4, tok4, tok4, spec_p, spec_s,
                      pl.BlockSpec(mod.shape, lambda i, *_: (0, 0, 0)),
                      pl.BlockSpec((1, d), lambda i, *_: (0, 0))],
            out_specs=[spec_p, spec_s],
            scratch_shapes=[pltpu.VMEM((2, max(COMBINE_SLOTS, N_EXPERTS) * COMBINE_WINDOW // 2, d),
                                       jnp.uint32),
                            pltpu.SemaphoreType.DMA((2,))]),
        out_shape=[jax.ShapeDtypeStruct((n_p, d), F32), jax.ShapeDtypeStruct((n_s, d), F32)],
        compiler_params=_params("arbitrary"),
        name="combine",
    )((slot_start // 2).reshape(-1), (fb_start // 2).reshape(-1), fb_passes,
      ys, col, idx, loc, prob, x1_p, x1_s, mod, g_final)


def _rope_tables(t, pos0):
    half = RET_QK_DIM // 2
    inv = ROPE_BASE ** (-jnp.arange(0, RET_QK_DIM, 2, dtype=F32) / RET_QK_DIM)
    pos = jnp.arange(t, dtype=F32) + float(pos0)
    ang = pos[:, None] * inv[None, :]
    assert ang.shape == (t, half)
    return jnp.cos(ang), jnp.sin(ang)


def _mixer(x, mod, pos0, conv_cache, s0, wts):
    bsz, t, d = x.shape
    cos, sin = _rope_tables(t, pos0)
    q, k, v, gs, cb, u, sga, sgb = _in_proj(x, mod, wts["g1"], cos, sin, wts["w_in"])
    yc, conv_new = _conv(u, cb, conv_cache, wts["conv_w"], wts["w_conv_out"])
    x1, xn2, s_new = _ret(q, k, v, gs, yc, sga, sgb, x, mod, s0,
                          wts["w_ret_out"], wts["w_mix_out"], wts["g2"])
    return x1.reshape(bsz * t, d), xn2.reshape(bsz * t, d), conv_new[None], s_new[None]


def _moe(x1_p, xn2_p, x1_s, xn2_s, mod, t_p, t_s, wts):
    n = x1_p.shape[0] + x1_s.shape[0]
    idx, prob, rank, tile_base, counts = _router(xn2_p, xn2_s, wts["w_router"], wts["b_router"])
    counts = counts.reshape(N_EXPERTS)
    *schedule, off = _visit_schedule(counts, n * TOP_K, EXPERT_ROWS)
    experts = jnp.arange(N_EXPERTS, dtype=jnp.int32)
    pick = lambda table: jnp.sum(jnp.where(idx[..., None] == experts, table, 0), axis=-1)
    pos = pick(off[:-1]) + rank
    xs = _dispatch(xn2_p, xn2_s, pos.reshape(n * TOP_K))
    ys = _experts(xs, schedule, wts["w_gate"], wts["b_gate"], wts["w_up"], wts["b_up"],
                  wts["w_down"], wts["b_down"])
    plan = _combine_plan(pos, idx, off, counts, tile_base, n * TOP_K)
    return _combine(ys, plan, idx, prob, x1_p, x1_s, mod, t_p, t_s, wts["g_final"])


def kernel(x_prompt, x_sample, c_prompt, c_sample, cache_conv, state_ret, w_ada, b_ada, g_norm1,
           w_in, conv_w, w_ret_out, w_conv_out, w_mix_out, g_norm2, w_router, b_router,
           w_gate, b_gate, w_up, b_up, w_down, b_down, g_final):
    assert w_ada.shape[0] == 1, "single-layer model"
    d = D_MODEL
    bp = x_prompt.shape[0]
    bs = x_sample.shape[0]
    wts = {
        "g1": g_norm1[0].reshape(1, d),
        "w_in": w_in[0].astype(BF16),
        "conv_w": conv_w[0],
        "w_ret_out": w_ret_out[0].astype(BF16),
        "w_conv_out": w_conv_out[0].astype(BF16),
        "w_mix_out": w_mix_out[0].astype(BF16),
        "g2": g_norm2[0].reshape(1, d),
        "w_router": w_router[0].astype(BF16),
        "b_router": b_router[0],
        "w_gate": w_gate[0],
        "b_gate": b_gate[0].reshape(N_EXPERTS, 1, d),
        "w_up": w_up[0],
        "b_up": b_up[0].reshape(N_EXPERTS, 1, d),
        "w_down": w_down[0],
        "b_down": b_down[0].reshape(N_EXPERTS, 1, d),
        "g_final": g_final.reshape(1, d),
    }
    mod = _ada(jnp.concatenate([c_prompt, c_sample], axis=0), w_ada[0], b_ada[0])
    mod = mod.reshape(bp + bs, 6, d)
    conv0 = jnp.zeros((bp, CONV_K - 1, d), F32)
    ret0 = jnp.zeros((bp, RET_HEADS, RET_QK_DIM, RET_V_DIM), F32)
    t_p, t_s = x_prompt.shape[1], x_sample.shape[1]
    x1_p, xn2_p, conv_p, ret_p = _mixer(x_prompt, mod[:bp], 0, conv0, ret0, wts)
    x1_s, xn2_s, conv_s, ret_s = _mixer(x_sample, mod[bp:], PAST_LEN, cache_conv[0], state_ret[0],
                                        wts)
    y_p, y_s = _moe(x1_p, xn2_p, x1_s, xn2_s, mod, t_p, t_s, wts)
    return (y_p.reshape(x_prompt.shape), y_s.reshape(x_sample.shape), conv_p, ret_p, conv_s, ret_s)
```

```python
import functools

import jax
import jax.numpy as jnp
import numpy as np
from jax import lax
from jax.experimental import pallas as pl
from jax.experimental.pallas import tpu as pltpu

D_MODEL = 1024
CHUNK = 64
RET_HEADS = 4
RET_QK_DIM = 256
RET_V_DIM = 512
RET_QK = RET_HEADS * RET_QK_DIM
RET_V = RET_HEADS * RET_V_DIM
CONV_K = 3
N_EXPERTS = 32
TOP_K = 4
SWIGLU_LIMIT = 7.0
SWIGLU_ALPHA = 1.702
ROPE_BASE = 10000.0
PAST_LEN = 4096
EPS = 1e-6
PROJ_WIDTH = 2 * RET_QK + 2 * RET_V + 5 * D_MODEL

V7X_VMEM_LIMIT_BYTES = 56 * 1024 * 1024
RET_BLOCK = 256
PROJ_ROWS = 256
TOK_TILE = 256
EXPERT_ROWS = 512
COMBINE_WINDOW = 64
COMBINE_SLOTS = 40
BF16_SUBLANES = 16
F32_SUBLANES = 8
VISIT_FIRST, VISIT_MERGE, VISIT_ZERO = 1, 0, 2

F32 = jnp.float32
BF16 = jnp.bfloat16


def _params(*sem):
    return pltpu.CompilerParams(dimension_semantics=sem, vmem_limit_bytes=V7X_VMEM_LIMIT_BYTES)


def _sigmoid(x):
    return 1.0 / (1.0 + jnp.exp(-x))


def _rms(x):
    return x * lax.rsqrt(jnp.mean(x * x, axis=-1, keepdims=True) + EPS)


def _ada_kernel(c_ref, w_ref, b_ref, o_ref):
    c = c_ref[...]
    s = c * _sigmoid(c)
    o_ref[...] = jnp.dot(s, w_ref[...], preferred_element_type=F32,
                         precision=lax.Precision.HIGHEST) + b_ref[...]


def _ada(c, w, b):
    n, d = c.shape
    width = w.shape[1]
    tn = 1024
    return pl.pallas_call(
        _ada_kernel,
        grid=(width // tn,),
        in_specs=[pl.BlockSpec((n, d), lambda j: (0, 0)),
                  pl.BlockSpec((d, tn), lambda j: (0, j)),
                  pl.BlockSpec((1, tn), lambda j: (0, j))],
        out_specs=pl.BlockSpec((n, tn), lambda j: (0, j)),
        out_shape=jax.ShapeDtypeStruct((n, width), F32),
        compiler_params=_params("arbitrary"),
        name="ada",
    )(c, w, b.reshape(1, width))


def _in_proj_kernel(x_ref, mod_ref, g1_ref, cos_ref, sin_ref, w_ref,
                    q_ref, k_ref, v_ref, gs_ref, cb_ref, u_ref, sga_ref, sgb_ref):
    x = x_ref[0]
    mod = mod_ref[0]
    xn = _rms(x) * g1_ref[...]
    xn = (xn * (1.0 + mod[1:2]) + mod[0:1]).astype(BF16)
    cos = cos_ref[...]
    sin = sin_ref[...]

    def proj(c0, width):
        return jnp.dot(xn, w_ref[:, c0:c0 + width], preferred_element_type=F32)

    def rope_store(acc, out_ref, scale):
        half = RET_QK_DIM // 2
        for h in range(RET_HEADS):
            a = acc[:, h * RET_QK_DIM: h * RET_QK_DIM + half]
            b = acc[:, h * RET_QK_DIM + half: (h + 1) * RET_QK_DIM]
            out_ref[0, :, h * RET_QK_DIM: h * RET_QK_DIM + half] = (
                (a * cos - b * sin) * scale).astype(BF16)
            out_ref[0, :, h * RET_QK_DIM + half: (h + 1) * RET_QK_DIM] = (
                (a * sin + b * cos) * scale).astype(BF16)

    c0 = 0
    rope_store(proj(c0, RET_QK), q_ref, 1.0)
    c0 += RET_QK
    rope_store(proj(c0, RET_QK), k_ref, RET_QK_DIM ** -0.5)
    c0 += RET_QK
    for j in range(RET_V // 1024):
        v_ref[0, :, j * 1024:(j + 1) * 1024] = proj(c0 + j * 1024, 1024).astype(BF16)
    c0 += RET_V
    for j in range(RET_V // 1024):
        g = proj(c0 + j * 1024, 1024)
        gs_ref[0, :, j * 1024:(j + 1) * 1024] = (g * _sigmoid(g)).astype(BF16)
    c0 += RET_V
    cb_ref[0] = proj(c0, D_MODEL).astype(BF16)
    c0 += D_MODEL
    cc = proj(c0, D_MODEL)
    c0 += D_MODEL
    u_ref[0] = cc * proj(c0, D_MODEL)
    c0 += D_MODEL
    sga_ref[0] = _sigmoid(proj(c0, D_MODEL)).astype(BF16)
    c0 += D_MODEL
    sgb_ref[0] = _sigmoid(proj(c0, D_MODEL)).astype(BF16)


def _in_proj(x, mod, g1, cos, sin, w_in_bf16):
    bsz, t, d = x.shape
    tm = min(PROJ_ROWS, t)
    row = lambda width: pl.BlockSpec((1, tm, width), lambda b, i: (b, i, 0))
    shp = lambda width, dt: jax.ShapeDtypeStruct((bsz, t, width), dt)
    return pl.pallas_call(
        _in_proj_kernel,
        grid=(bsz, t // tm),
        in_specs=[row(d),
                  pl.BlockSpec((1, 6, d), lambda b, i: (b, 0, 0)),
                  pl.BlockSpec((1, d), lambda b, i: (0, 0)),
                  pl.BlockSpec((tm, RET_QK_DIM // 2), lambda b, i: (i, 0)),
                  pl.BlockSpec((tm, RET_QK_DIM // 2), lambda b, i: (i, 0)),
                  pl.BlockSpec((d, PROJ_WIDTH), lambda b, i: (0, 0),
                               pipeline_mode=pl.Buffered(1))],
        out_specs=[row(RET_QK), row(RET_QK), row(RET_V), row(RET_V),
                   row(d), row(d), row(d), row(d)],
        out_shape=[shp(RET_QK, BF16), shp(RET_QK, BF16), shp(RET_V, BF16), shp(RET_V, BF16),
                   shp(d, BF16), shp(d, F32), shp(d, BF16), shp(d, BF16)],
        compiler_params=_params("parallel", "parallel"),
        name="in_proj",
    )(x, mod, g1, cos, sin, w_in_bf16)


def _conv_kernel(u_ref, cb_ref, cache_ref, cw_ref, w_ref, y_ref, new_ref, prev_ref):
    i = pl.program_id(1)

    @pl.when(i == 0)
    def _():
        prev_ref[...] = cache_ref[0]

    u = u_ref[0]
    tc = u.shape[0]
    prev = prev_ref[...]
    rows = lax.broadcasted_iota(jnp.int32, u.shape, 0)
    u1 = jnp.where(rows == 0, prev[1:2], pltpu.roll(u, 1, axis=0))
    u2 = jnp.where(rows == 0, prev[0:1], jnp.where(rows == 1, prev[1:2], pltpu.roll(u, 2, axis=0)))
    cw = cw_ref[...]
    conv = u2 * cw[0:1] + u1 * cw[1:2] + u * cw[2:3]
    z = (cb_ref[0].astype(F32) * conv).astype(BF16)
    y_ref[0] = jnp.dot(z, w_ref[...], preferred_element_type=F32).astype(BF16)
    tail = u_ref[0, tc - 2:tc, :]
    prev_ref[...] = tail
    new_ref[0] = tail


def _conv(u, cb, cache, conv_w, w_conv_out_bf16):
    bsz, t, d = u.shape
    tc = min(512, t)
    return pl.pallas_call(
        _conv_kernel,
        grid=(bsz, t // tc),
        in_specs=[pl.BlockSpec((1, tc, d), lambda b, i: (b, i, 0)),
                  pl.BlockSpec((1, tc, d), lambda b, i: (b, i, 0)),
                  pl.BlockSpec((1, CONV_K - 1, d), lambda b, i: (b, 0, 0)),
                  pl.BlockSpec((CONV_K, d), lambda b, i: (0, 0)),
                  pl.BlockSpec((d, d), lambda b, i: (0, 0))],
        out_specs=[pl.BlockSpec((1, tc, d), lambda b, i: (b, i, 0)),
                   pl.BlockSpec((1, CONV_K - 1, d), lambda b, i: (b, 0, 0))],
        out_shape=[jax.ShapeDtypeStruct((bsz, t, d), BF16),
                   jax.ShapeDtypeStruct((bsz, CONV_K - 1, d), F32)],
        scratch_shapes=[pltpu.VMEM((CONV_K - 1, d), F32)],
        compiler_params=_params("parallel", "arbitrary"),
        name="conv",
    )(u, cb, cache, conv_w, w_conv_out_bf16)


def _decay_tables(block):
    lg = np.log(1.0 - np.exp2(-5.0 - np.arange(RET_HEADS, dtype=np.float64)))
    idx = np.arange(block, dtype=np.float64)
    dist = np.abs(idx[:, None] - idx[None, :])
    visible = (idx[None, :] // CHUNK) <= (idx[:, None] // CHUNK)
    dmat = np.where(visible[None], np.exp(dist[None] * lg[:, None, None]), 0.0)
    qdec = np.exp((idx + 1.0)[None, :] * lg[:, None])
    kdec = np.exp((block - 1.0 - idx)[None, :] * lg[:, None])
    sdec = np.exp(block * lg)
    lanes = lambda a: jnp.asarray(np.broadcast_to(a[:, :, None], (RET_HEADS, block, 128)), F32)
    return jnp.asarray(dmat, F32), lanes(qdec), lanes(kdec), [float(s) for s in sdec]


def _ret_kernel(sdec, q_ref, k_ref, v_ref, gs_ref, yc_ref, sga_ref, sgb_ref, x_ref, mod_ref,
                s0_ref, dmat_ref, qdec_ref, kdec_ref, wro_ref, wmo_ref, g2_ref,
                x1_ref, xn2_ref, s_ref):
    i = pl.program_id(1)

    @pl.when(i == 0)
    def _():
        s_ref[...] = s0_ref[...]

    ls = q_ref.shape[1]
    yret = jnp.zeros((ls, D_MODEL), F32)
    for h in range(RET_HEADS):
        qh = q_ref[0, :, h * RET_QK_DIM:(h + 1) * RET_QK_DIM]
        kh = k_ref[0, :, h * RET_QK_DIM:(h + 1) * RET_QK_DIM]
        vh = v_ref[0, :, h * RET_V_DIM:(h + 1) * RET_V_DIM]
        state = s_ref[0, h]
        scores = lax.dot_general(qh, kh, (((1,), (1,)), ((), ())), preferred_element_type=F32)
        p = (scores * dmat_ref[h]).astype(BF16)
        o = jnp.dot(p, vh, preferred_element_type=F32)
        qdec = jnp.concatenate([qdec_ref[h]] * (RET_V_DIM // 128), axis=1)
        o = o + jnp.dot(qh, state.astype(BF16), preferred_element_type=F32) * qdec
        kdec = jnp.concatenate([kdec_ref[h]] * (RET_QK_DIM // 128), axis=1)
        kd = (kh.astype(F32) * kdec).astype(BF16)
        s_ref[0, h] = sdec[h] * state + lax.dot_general(
            kd, vh, (((0,), (0,)), ((), ())), preferred_element_type=F32)
        og = (gs_ref[0, :, h * RET_V_DIM:(h + 1) * RET_V_DIM].astype(F32) * _rms(o)).astype(BF16)
        yret = yret + jnp.dot(og, wro_ref[h * RET_V_DIM:(h + 1) * RET_V_DIM, :],
                              preferred_element_type=F32)
    hmix = (sga_ref[0].astype(F32) * yc_ref[0].astype(F32) + sgb_ref[0].astype(F32) * yret)
    mix = jnp.dot(hmix.astype(BF16), wmo_ref[...], preferred_element_type=F32)
    mod = mod_ref[0]
    x1 = x_ref[0] + mod[2:3] * mix
    x1_ref[0] = x1
    xn2 = _rms(x1) * g2_ref[...]
    xn2_ref[0] = xn2 * (1.0 + mod[4:5]) + mod[3:4]


def _ret(q, k, v, gs, yc, sga, sgb, x, mod, s0, w_ret_out_bf16, w_mix_out_bf16, g2):
    bsz, t, d = x.shape
    ls = min(RET_BLOCK, t)
    dmat, qdec, kdec, sdec = _decay_tables(ls)
    row = lambda width: pl.BlockSpec((1, ls, width), lambda b, i: (b, i, 0))
    const = lambda shape: pl.BlockSpec(shape, lambda b, i: (0,) * len(shape))
    state_spec = pl.BlockSpec((1, RET_HEADS, RET_QK_DIM, RET_V_DIM), lambda b, i: (b, 0, 0, 0))
    return pl.pallas_call(
        functools.partial(_ret_kernel, sdec),
        grid=(bsz, t // ls),
        in_specs=[row(RET_QK), row(RET_QK), row(RET_V), row(RET_V), row(d), row(d), row(d), row(d),
                  pl.BlockSpec((1, 6, d), lambda b, i: (b, 0, 0)),
                  state_spec,
                  const((RET_HEADS, ls, ls)), const((RET_HEADS, ls, 128)),
                  const((RET_HEADS, ls, 128)),
                  const((RET_V, d)), const((d, d)), const((1, d))],
        out_specs=[row(d), row(d), state_spec],
        out_shape=[jax.ShapeDtypeStruct((bsz, t, d), F32),
                   jax.ShapeDtypeStruct((bsz, t, d), F32),
                   jax.ShapeDtypeStruct(s0.shape, F32)],
        compiler_params=_params("parallel", "arbitrary"),
        name="ret",
    )(q, k, v, gs, yc, sga, sgb, x, mod, s0, dmat, qdec, kdec, w_ret_out_bf16, w_mix_out_bf16, g2)


def _two_stream_specs(n_p, n_s, tm, width):
    tiles_p = n_p // tm
    last_p = tiles_p - 1
    return (pl.BlockSpec((tm, width), lambda i: (jnp.minimum(i, last_p), 0)),
            pl.BlockSpec((tm, width), lambda i: (jnp.maximum(i - tiles_p, 0), 0)))


def _router_kernel(tiles_p, xp_ref, xs_ref, wt_ref, b_ref, idx_ref, prob_ref, rank_ref, base_ref,
                   cnt_ref, run_ref):
    i = pl.program_id(0)

    @pl.when(i == 0)
    def _():
        run_ref[...] = jnp.zeros_like(run_ref)

    diag = (lax.broadcasted_iota(jnp.int32, (N_EXPERTS, N_EXPERTS), 0)
            == lax.broadcasted_iota(jnp.int32, (N_EXPERTS, N_EXPERTS), 1))
    as_row = lambda column: jnp.sum(jnp.where(diag, column, 0.0), axis=0, keepdims=True)
    base_ref[0] = as_row(run_ref[...]).astype(jnp.int32)

    xn = jnp.where(i < tiles_p, xp_ref[...], xs_ref[...]).astype(BF16)
    logits = lax.dot_general(wt_ref[...], xn, (((1,), (1,)), ((), ())),
                             preferred_element_type=F32) + b_ref[...]
    tm = logits.shape[1]
    expert = lax.broadcasted_iota(jnp.int32, logits.shape, 0)
    slot = lax.broadcasted_iota(jnp.int32, (TOP_K, tm), 0)
    work = logits
    sels, vals = [], []
    idx = jnp.zeros((TOP_K, tm), jnp.int32)
    for k in range(TOP_K):
        m = jnp.max(work, axis=0, keepdims=True)
        first = jnp.min(jnp.where(work == m, expert, N_EXPERTS), axis=0, keepdims=True)
        sel = expert == first
        sels.append(sel)
        vals.append(m)
        idx = jnp.where(slot == k, first, idx)
        work = jnp.where(sel, -jnp.inf, work)
    exps = [jnp.exp(v - vals[0]) for v in vals]
    denom = exps[0] + exps[1] + exps[2] + exps[3]
    prob = jnp.zeros((TOP_K, tm), F32)
    for k in range(TOP_K):
        prob = jnp.where(slot == k, exps[k] / denom, prob)
    member = jnp.zeros(logits.shape, F32)
    for sel in sels:
        member = member + jnp.where(sel, 1.0, 0.0)
    src = lax.broadcasted_iota(jnp.int32, (tm, tm), 0)
    dst = lax.broadcasted_iota(jnp.int32, (tm, tm), 1)
    earlier = jnp.where(src < dst, 1.0, 0.0).astype(BF16)
    before = jnp.dot(member.astype(BF16), earlier, preferred_element_type=F32) + run_ref[...]
    rank = jnp.zeros((TOP_K, tm), jnp.int32)
    for k in range(TOP_K):
        rk = jnp.sum(jnp.where(sels[k], before, 0.0), axis=0, keepdims=True)
        rank = jnp.where(slot == k, rk.astype(jnp.int32), rank)
    run_ref[...] += jnp.sum(member, axis=1, keepdims=True)
    idx_ref[...] = idx
    prob_ref[...] = prob
    rank_ref[...] = rank
    cnt_ref[...] = as_row(run_ref[...]).astype(jnp.int32)


def _router(xn2_p, xn2_s, w_router_t_bf16, b_router):
    (n_p, d), n_s = xn2_p.shape, xn2_s.shape[0]
    tm = TOK_TILE
    n = n_p + n_s
    spec_p, spec_s = _two_stream_specs(n_p, n_s, tm, d)
    pairs = pl.BlockSpec((TOP_K, tm), lambda i: (0, i))
    return pl.pallas_call(
        functools.partial(_router_kernel, n_p // tm),
        grid=(n // tm,),
        in_specs=[spec_p, spec_s,
                  pl.BlockSpec((N_EXPERTS, d), lambda i: (0, 0)),
                  pl.BlockSpec((N_EXPERTS, 1), lambda i: (0, 0))],
        out_specs=[pairs, pairs, pairs,
                   pl.BlockSpec((1, 1, N_EXPERTS), lambda i: (i, 0, 0)),
                   pl.BlockSpec((1, N_EXPERTS), lambda i: (0, 0))],
        out_shape=[jax.ShapeDtypeStruct((TOP_K, n), jnp.int32),
                   jax.ShapeDtypeStruct((TOP_K, n), F32),
                   jax.ShapeDtypeStruct((TOP_K, n), jnp.int32),
                   jax.ShapeDtypeStruct((n // tm, 1, N_EXPERTS), jnp.int32),
                   jax.ShapeDtypeStruct((1, N_EXPERTS), jnp.int32)],
        scratch_shapes=[pltpu.VMEM((N_EXPERTS, 1), F32)],
        compiler_params=_params("arbitrary"),
        name="router",
    )(xn2_p, xn2_s, w_router_t_bf16, b_router.reshape(N_EXPERTS, 1))


def _dispatch_kernel(tiles_p, pos_hbm, xp_ref, xs_ref, out_hbm, pos_smem, row_buf, sem_pos, sem_rows):
    i = pl.program_id(0)
    n = pl.num_programs(0)
    groups, sub, d = row_buf.shape[1:]
    tm = groups * sub
    slot = i % 2

    def pos_copy(step, s):
        return pltpu.make_async_copy(pos_hbm.at[pl.ds(step * (tm * TOP_K), tm * TOP_K)],
                                     pos_smem.at[pl.ds(s * (tm * TOP_K), tm * TOP_K)],
                                     sem_pos.at[s])

    def drain(s):
        for _ in range(TOP_K):
            pltpu.make_async_copy(xp_ref, out_hbm.at[pl.ds(0, tm)], sem_rows.at[s]).wait()

    @pl.when(i == 0)
    def _():
        pos_copy(0, 0).start()

    @pl.when(i + 1 < n)
    def _():
        pos_copy(i + 1, 1 - slot).start()

    @pl.when(i >= 2)
    def _():
        drain(slot)

    row_buf[slot] = jnp.where(i < tiles_p, xp_ref[...], xs_ref[...]).reshape(groups, sub, d)
    pos_copy(i, slot).wait()
    pos_base = slot * (tm * TOP_K)

    def issue(g, carry):
        for u in range(sub):
            for k in range(TOP_K):
                dst = pos_smem[pos_base + g * sub + (k * tm + u)]
                pltpu.make_async_copy(row_buf.at[slot, g, pl.ds(u, 1)], out_hbm.at[pl.ds(dst, 1)],
                                      sem_rows.at[slot]).start(priority=k % 2)
        return carry

    lax.fori_loop(0, groups, issue, 0)

    @pl.when(i == n - 1)
    def _():
        drain(slot)

    @pl.when((i == n - 1) & (n >= 2))
    def _():
        drain(1 - slot)


def _dispatch(xn2_p, xn2_s, pos_flat):
    (n_p, d), n_s = xn2_p.shape, xn2_s.shape[0]
    tm = TOK_TILE
    n = n_p + n_s
    spec_p, spec_s = _two_stream_specs(n_p, n_s, tm, d)
    return pl.pallas_call(
        functools.partial(_dispatch_kernel, n_p // tm),
        grid=(n // tm,),
        in_specs=[pl.BlockSpec(memory_space=pl.ANY), spec_p, spec_s],
        out_specs=pl.BlockSpec(memory_space=pl.ANY),
        out_shape=jax.ShapeDtypeStruct((n * TOP_K, d), F32),
        scratch_shapes=[pltpu.SMEM((2 * tm * TOP_K,), jnp.int32),
                        pltpu.VMEM((2, tm // F32_SUBLANES, F32_SUBLANES, d), F32),
                        pltpu.SemaphoreType.DMA((2,)),
                        pltpu.SemaphoreType.DMA((2,))],
        compiler_params=_params("arbitrary"),
        name="dispatch",
    )(pos_flat, xn2_p, xn2_s)


def _visit_schedule(counts, n_rows, tm):
    n_tiles = n_rows // tm
    n_visits = n_tiles + N_EXPERTS
    off = jnp.concatenate([jnp.zeros((1,), jnp.int32), jnp.cumsum(counts, dtype=jnp.int32)])
    first_tile = off[:-1] // tm
    last_tile = (off[1:] - 1) // tm
    per_expert = jnp.where(counts > 0, last_tile - first_tile + 1, 0)
    ends = jnp.cumsum(per_expert, dtype=jnp.int32)
    total = ends[-1]
    step = jnp.arange(n_visits, dtype=jnp.int32)
    v = jnp.minimum(step, total - 1)
    expert = jnp.sum((ends[None, :] <= v[:, None]).astype(jnp.int32), axis=1)
    mine = expert[:, None] == jnp.arange(N_EXPERTS, dtype=jnp.int32)[None, :]
    pick = lambda table: jnp.sum(jnp.where(mine, table[None, :], 0), axis=1)
    live = step < total
    tile = jnp.where(live, pick(first_tile) + (v - pick(ends - per_expert)), n_tiles)
    lo = jnp.where(live, jnp.clip(pick(off[:-1]) - tile * tm, 0, tm), 0)
    hi = jnp.where(live, jnp.clip(pick(off[1:]) - tile * tm, 0, tm), 0)
    changed = jnp.concatenate([jnp.ones((1,), jnp.bool_), tile[1:] != tile[:-1]])
    mode = jnp.where(live, jnp.where(changed, VISIT_FIRST, VISIT_MERGE),
                     jnp.where(step == total, VISIT_ZERO, VISIT_MERGE)).astype(jnp.int32)
    return tile, expert, lo, hi, mode, off


def _expert_kernel(tile_ref, expert_ref, lo_ref, hi_ref, mode_ref,
                   x_ref, wg_ref, bg_ref, wu_ref, bu_ref, wd_ref, bd_ref, y_ref, w_bf16):
    v = pl.program_id(0)
    lo = lo_ref[v]
    hi = hi_ref[v]

    @pl.when((v == 0) | (expert_ref[v] != expert_ref[jnp.maximum(v - 1, 0)]))
    def _():
        w_bf16[0] = wg_ref[0].astype(BF16)
        w_bf16[1] = wu_ref[0].astype(BF16)
        w_bf16[2] = wd_ref[0].astype(BF16)

    @pl.when(mode_ref[v] == VISIT_ZERO)
    def _():
        y_ref[...] = jnp.zeros_like(y_ref)

    @pl.when(hi > lo)
    def _():
        xn = x_ref[...].astype(BF16)
        gl = jnp.minimum(jnp.dot(xn, w_bf16[0], preferred_element_type=F32) + bg_ref[0],
                         SWIGLU_LIMIT)
        lin = jnp.clip(jnp.dot(xn, w_bf16[1], preferred_element_type=F32) + bu_ref[0],
                       -SWIGLU_LIMIT, SWIGLU_LIMIT)
        act = gl * _sigmoid(SWIGLU_ALPHA * gl) * (lin + 1.0)
        out = jnp.dot(act.astype(BF16), w_bf16[2], preferred_element_type=F32) + bd_ref[0]
        row = lax.broadcasted_iota(jnp.int32, out.shape, 0)
        mine = (row >= lo) & (row < hi)

        @pl.when(mode_ref[v] == VISIT_FIRST)
        def _():
            y_ref[...] = pltpu.bitcast(jnp.where(mine, out, 0.0).astype(BF16), jnp.uint32)

        @pl.when(mode_ref[v] == VISIT_MERGE)
        def _():
            prev = pltpu.bitcast(y_ref[...], BF16)
            y_ref[...] = pltpu.bitcast(jnp.where(mine, out.astype(BF16), prev), jnp.uint32)


def _experts(xs, schedule, wg, bg, wu, bu, wd, bd):
    n_rows, d = xs.shape
    tm = EXPERT_ROWS
    last_in = n_rows // tm - 1
    tile, expert, lo, hi, mode = schedule
    rows_in = pl.BlockSpec((tm, d), lambda v, tile, *_: (jnp.minimum(tile[v], last_in), 0))
    rows_out = pl.BlockSpec((tm // 2, d), lambda v, tile, *_: (tile[v], 0))
    wspec = pl.BlockSpec((1, d, d), lambda v, tile, expert, *_: (expert[v], 0, 0))
    bspec = pl.BlockSpec((1, 1, d), lambda v, tile, expert, *_: (expert[v], 0, 0))
    return pl.pallas_call(
        _expert_kernel,
        grid_spec=pltpu.PrefetchScalarGridSpec(
            num_scalar_prefetch=5,
            grid=(tile.shape[0],),
            in_specs=[rows_in, wspec, bspec, wspec, bspec, wspec, bspec],
            out_specs=rows_out,
            scratch_shapes=[pltpu.VMEM((3, d, d), BF16)]),
        out_shape=jax.ShapeDtypeStruct(((n_rows + tm) // 2, d), jnp.uint32),
        compiler_params=_params("arbitrary"),
        name="experts",
    )(tile, expert, lo, hi, mode, xs, wg, bg, wu, bu, wd, bd)


def _combine_plan(pos, idx, off, counts, tile_base, n_rows):
    w, slots = COMBINE_WINDOW, COMBINE_SLOTS
    base = tile_base.reshape(-1, N_EXPERTS)
    n_tiles = base.shape[0]
    seg_start = off[:-1][None, :] + base
    n_seg = jnp.concatenate([base[1:], counts[None, :]], axis=0) - base
    win_start = (seg_start // BF16_SUBLANES) * BF16_SUBLANES
    n_win = jnp.where(n_seg > 0, (seg_start - win_start + n_seg + w - 1) // w, 0)
    slot_end = jnp.cumsum(n_win, axis=1, dtype=jnp.int32)
    slot_base = slot_end - n_win
    experts = jnp.arange(N_EXPERTS, dtype=jnp.int32)
    per_token = lambda table: jnp.repeat(table.T, TOK_TILE, axis=1)[:, None, :]
    pick = lambda table: jnp.sum(
        jnp.where(idx[None] == experts[:, None, None], per_token(table), 0), axis=0)
    loc = pos - pick(win_start)
    slot = pick(slot_base) + loc // w
    col = jnp.where(slot < slots, slot * w + loc % w, -1)
    q = jnp.arange(slots, dtype=jnp.int32)[None, :, None]
    e_q = jnp.sum((slot_end[:, None, :] <= q).astype(jnp.int32), axis=-1)
    of_slot = lambda table: jnp.sum(jnp.where(e_q[..., None] == experts, table[:, None, :], 0), -1)
    used = q[..., 0] < slot_end[:, -1:]
    slot_start = jnp.where(used, of_slot(win_start) + (q[..., 0] - of_slot(slot_base)) * w, n_rows)
    fb_pass = jnp.where(col < 0, loc // w + 1, 0).reshape(TOP_K, n_tiles, TOK_TILE)
    return col, slot_start, win_start, loc, jnp.max(fb_pass, axis=(0, 2))


def _combine_kernel(tiles_p, n_p, t_p, b_p, t_s, slot_ref, fb_ref, npass_ref, ys_hbm, col_ref,
                    idx_ref, loc_ref, prob_ref, x1p_ref, x1s_ref, mod_ref, gf_ref, yp_ref, ysm_ref,
                    win_buf, sem):
    i = pl.program_id(0)
    n = pl.num_programs(0)
    tm = idx_ref.shape[0]
    slot = i % 2
    w_rows = COMBINE_WINDOW
    w_words = w_rows // 2

    def fetch(starts_ref, count, first, shift, s):
        for j in range(count):
            start = pl.multiple_of(starts_ref[first + j] + shift, F32_SUBLANES)
            pltpu.make_async_copy(ys_hbm.at[pl.ds(start, w_words)],
                                  win_buf.at[s, pl.ds(j * w_words, w_words)],
                                  sem.at[s]).start(priority=j % 2)

    def wait(count, s):
        pltpu.make_async_copy(ys_hbm.at[pl.ds(0, count * w_words)],
                              win_buf.at[s, pl.ds(0, count * w_words)], sem.at[s]).wait()

    def mix(cols, count, s):
        lane = lax.broadcasted_iota(jnp.int32, (tm, count * w_rows), 1)
        prob = prob_ref[...]
        coef = jnp.zeros(lane.shape, F32)
        for k in range(TOP_K):
            coef = jnp.where(lane == cols[:, k:k + 1], prob[:, k:k + 1], coef)
        rows = pltpu.bitcast(win_buf[s, 0:count * w_words, :], BF16)
        return jnp.dot(coef.astype(BF16), rows, preferred_element_type=F32)

    @pl.when(i == 0)
    def _():
        fetch(slot_ref, COMBINE_SLOTS, 0, 0, 0)

    @pl.when(i + 1 < n)
    def _():
        fetch(slot_ref, COMBINE_SLOTS, (i + 1) * COMBINE_SLOTS, 0, 1 - slot)

    wait(COMBINE_SLOTS, slot)
    covered = col_ref[...]
    moe = mix(covered, COMBINE_SLOTS, slot)

    def fallback_pass(p, acc):
        fetch(fb_ref, N_EXPERTS, i * N_EXPERTS, p * w_words, slot)
        wait(N_EXPERTS, slot)
        lw = loc_ref[...] - p * w_rows
        cols = jnp.where((covered < 0) & (lw >= 0) & (lw < w_rows), idx_ref[...] * w_rows + lw, -1)
        return acc + mix(cols, N_EXPERTS, slot)

    moe = lax.fori_loop(0, npass_ref[i], fallback_pass, moe)
    x1 = jnp.where(i < tiles_p, x1p_ref[...], x1s_ref[...])
    gate_rows = []
    for j in range(tm // CHUNK):
        n0 = i * tm + j * CHUNK
        b = jnp.where(n0 < n_p, n0 // t_p, b_p + (n0 - n_p) // t_s)
        gate_rows.append(jnp.broadcast_to(mod_ref[b, 5:6, :], (CHUNK, D_MODEL)))
    x2 = x1 + jnp.concatenate(gate_rows, axis=0) * moe
    y = _rms(x2) * gf_ref[...]

    @pl.when(i < tiles_p)
    def _():
        yp_ref[...] = y

    @pl.when(i >= tiles_p)
    def _():
        ysm_ref[...] = y


def _combine(ys, plan, idx, prob, x1_p, x1_s, mod, t_p, t_s, g_final):
    col, slot_start, fb_start, loc, fb_passes = plan
    col, idx, loc, prob = col.T, idx.T, loc.T, prob.T
    (n_p, d), n_s = x1_p.shape, x1_s.shape[0]
    tm = TOK_TILE
    n = n_p + n_s
    assert t_p % CHUNK == 0 and t_s % CHUNK == 0 and tm % CHUNK == 0
    tiles_p = n_p // tm
    last_p = tiles_p - 1
    spec_p = pl.BlockSpec((tm, d), lambda i, *_: (jnp.minimum(i, last_p), 0))
    spec_s = pl.BlockSpec((tm, d), lambda i, *_: (jnp.maximum(i - tiles_p, 0), 0))
    tok4 = pl.BlockSpec((tm, TOP_K), lambda i, *_: (i, 0))
    return pl.pallas_call(
        functools.partial(_combine_kernel, tiles_p, n_p, t_p, n_p // t_p, t_s),
        grid_spec=pltpu.PrefetchScalarGridSpec(
            num_scalar_prefetch=3,
            grid=(n // tm,),
            in_specs=[pl.BlockSpec(memory_space=pl.ANY), tok4, tok4, tok4, tok4, spec_p, spec_s,
                      pl.BlockSpec(mod.shape, lambda i, *_: (0, 0, 0)),
                      pl.BlockSpec((1, d), lambda i, *_: (0, 0))],
            out_specs=[spec_p, spec_s],
            scratch_shapes=[pltpu.VMEM((2, max(COMBINE_SLOTS, N_EXPERTS) * COMBINE_WINDOW // 2, d),
                                       jnp.uint32),
                            pltpu.SemaphoreType.DMA((2,))]),
        out_shape=[jax.ShapeDtypeStruct((n_p, d), F32), jax.ShapeDtypeStruct((n_s, d), F32)],
        compiler_params=_params("arbitrary"),
        name="combine",
    )((slot_start // 2).reshape(-1), (fb_start // 2).reshape(-1), fb_passes,
      ys, col, idx, loc, prob, x1_p, x1_s, mod, g_final)


def _rope_tables(t, pos0):
    half = RET_QK_DIM // 2
    inv = ROPE_BASE ** (-jnp.arange(0, RET_QK_DIM, 2, dtype=F32) / RET_QK_DIM)
    pos = jnp.arange(t, dtype=F32) + float(pos0)
    ang = pos[:, None] * inv[None, :]
    assert ang.shape == (t, half)
    return jnp.cos(ang), jnp.sin(ang)


def _mixer(x, mod, pos0, conv_cache, s0, wts):
    bsz, t, d = x.shape
    cos, sin = _rope_tables(t, pos0)
    q, k, v, gs, cb, u, sga, sgb = _in_proj(x, mod, wts["g1"], cos, sin, wts["w_in"])
    yc, conv_new = _conv(u, cb, conv_cache, wts["conv_w"], wts["w_conv_out"])
    x1, xn2, s_new = _ret(q, k, v, gs, yc, sga, sgb, x, mod, s0,
                          wts["w_ret_out"], wts["w_mix_out"], wts["g2"])
    return x1.reshape(bsz * t, d), xn2.reshape(bsz * t, d), conv_new[None], s_new[None]


def _moe(x1_p, xn2_p, x1_s, xn2_s, mod, t_p, t_s, wts):
    n = x1_p.shape[0] + x1_s.shape[0]
    idx, prob, rank, tile_base, counts = _router(xn2_p, xn2_s, wts["w_router"], wts["b_router"])
    counts = counts.reshape(N_EXPERTS)
    *schedule, off = _visit_schedule(counts, n * TOP_K, EXPERT_ROWS)
    experts = jnp.arange(N_EXPERTS, dtype=jnp.int32)[:, None, None]
    pos = jnp.sum(jnp.where(idx[None] == experts, off[:-1, None, None], 0), axis=0) + rank
    pos_tiles = pos.reshape(TOP_K, n // TOK_TILE, TOK_TILE).transpose(1, 0, 2)
    xs = _dispatch(xn2_p, xn2_s, pos_tiles.reshape(n * TOP_K))
    ys = _experts(xs, schedule, wts["w_gate"], wts["b_gate"], wts["w_up"], wts["b_up"],
                  wts["w_down"], wts["b_down"])
    plan = _combine_plan(pos, idx, off, counts, tile_base, n * TOP_K)
    return _combine(ys, plan, idx, prob, x1_p, x1_s, mod, t_p, t_s, wts["g_final"])


def kernel(x_prompt, x_sample, c_prompt, c_sample, cache_conv, state_ret, w_ada, b_ada, g_norm1,
           w_in, conv_w, w_ret_out, w_conv_out, w_mix_out, g_norm2, w_router, b_router,
           w_gate, b_gate, w_up, b_up, w_down, b_down, g_final):
    assert w_ada.shape[0] == 1, "single-layer model"
    d = D_MODEL
    bp = x_prompt.shape[0]
    bs = x_sample.shape[0]
    wts = {
        "g1": g_norm1[0].reshape(1, d),
        "w_in": w_in[0].astype(BF16),
        "conv_w": conv_w[0],
        "w_ret_out": w_ret_out[0].astype(BF16),
        "w_conv_out": w_conv_out[0].astype(BF16),
        "w_mix_out": w_mix_out[0].astype(BF16),
        "g2": g_norm2[0].reshape(1, d),
        "w_router": w_router[0].T.astype(BF16),
        "b_router": b_router[0],
        "w_gate": w_gate[0],
        "b_gate": b_gate[0].reshape(N_EXPERTS, 1, d),
        "w_up": w_up[0],
        "b_up": b_up[0].reshape(N_EXPERTS, 1, d),
        "w_down": w_down[0],
        "b_down": b_down[0].reshape(N_EXPERTS, 1, d),
        "g_final": g_final.reshape(1, d),
    }
    mod = _ada(jnp.concatenate([c_prompt, c_sample], axis=0), w_ada[0], b_ada[0])
    mod = mod.reshape(bp + bs, 6, d)
    conv0 = jnp.zeros((bp, CONV_K - 1, d), F32)
    ret0 = jnp.zeros((bp, RET_HEADS, RET_QK_DIM, RET_V_DIM), F32)
    t_p, t_s = x_prompt.shape[1], x_sample.shape[1]
    x1_p, xn2_p, conv_p, ret_p = _mixer(x_prompt, mod[:bp], 0, conv0, ret0, wts)
    x1_s, xn2_s, conv_s, ret_s = _mixer(x_sample, mod[bp:], PAST_LEN, cache_conv[0], state_ret[0],
                                        wts)
    y_p, y_s = _moe(x1_p, xn2_p, x1_s, xn2_s, mod, t_p, t_s, wts)
    return (y_p.reshape(x_prompt.shape), y_s.reshape(x_sample.shape), conv_p, ret_p, conv_s, ret_s)
```

```python
import functools

import jax
import jax.numpy as jnp
import numpy as np
from jax import lax
from jax.experimental import pallas as pl
from jax.experimental.pallas import tpu as pltpu

D_MODEL = 1024
CHUNK = 64
RET_HEADS = 4
RET_QK_DIM = 256
RET_V_DIM = 512
RET_QK = RET_HEADS * RET_QK_DIM
RET_V = RET_HEADS * RET_V_DIM
CONV_K = 3
N_EXPERTS = 32
TOP_K = 4
SWIGLU_LIMIT = 7.0
SWIGLU_ALPHA = 1.702
ROPE_BASE = 10000.0
PAST_LEN = 4096
EPS = 1e-6
PROJ_WIDTH = 2 * RET_QK + 2 * RET_V + 5 * D_MODEL

V7X_VMEM_LIMIT_BYTES = 56 * 1024 * 1024
RET_BLOCK = 256
PROJ_ROWS = 256
TOK_TILE = 256
DISPATCH_TILE = 512
EXPERT_ROWS = 512
COMBINE_WINDOW = 32
COMBINE_SLOTS = 64
BF16_SUBLANES = 16
F32_SUBLANES = 8
VISIT_FIRST, VISIT_MERGE, VISIT_ZERO = 1, 0, 2

F32 = jnp.float32
BF16 = jnp.bfloat16


def _params(*sem):
    return pltpu.CompilerParams(dimension_semantics=sem, vmem_limit_bytes=V7X_VMEM_LIMIT_BYTES)


def _sigmoid(x):
    return 1.0 / (1.0 + jnp.exp(-x))


def _rms(x):
    return x * lax.rsqrt(jnp.mean(x * x, axis=-1, keepdims=True) + EPS)


def _ada_kernel(c_ref, w_ref, b_ref, o_ref):
    c = c_ref[...]
    s = c * _sigmoid(c)
    o_ref[...] = jnp.dot(s, w_ref[...], preferred_element_type=F32,
                         precision=lax.Precision.HIGHEST) + b_ref[...]


def _ada(c, w, b):
    n, d = c.shape
    width = w.shape[1]
    tn = 1024
    return pl.pallas_call(
        _ada_kernel,
        grid=(width // tn,),
        in_specs=[pl.BlockSpec((n, d), lambda j: (0, 0)),
                  pl.BlockSpec((d, tn), lambda j: (0, j)),
                  pl.BlockSpec((1, tn), lambda j: (0, j))],
        out_specs=pl.BlockSpec((n, tn), lambda j: (0, j)),
        out_shape=jax.ShapeDtypeStruct((n, width), F32),
        compiler_params=_params("arbitrary"),
        name="ada",
    )(c, w, b.reshape(1, width))


def _in_proj_kernel(x_ref, mod_ref, g1_ref, cos_ref, sin_ref, w_ref,
                    q_ref, k_ref, v_ref, gs_ref, cb_ref, u_ref, sga_ref, sgb_ref):
    x = x_ref[0]
    mod = mod_ref[0]
    xn = _rms(x) * g1_ref[...]
    xn = (xn * (1.0 + mod[1:2]) + mod[0:1]).astype(BF16)
    cos = cos_ref[...]
    sin = sin_ref[...]

    def proj(c0, width):
        return jnp.dot(xn, w_ref[:, c0:c0 + width], preferred_element_type=F32)

    def rope_store(acc, out_ref, scale):
        half = RET_QK_DIM // 2
        for h in range(RET_HEADS):
            a = acc[:, h * RET_QK_DIM: h * RET_QK_DIM + half]
            b = acc[:, h * RET_QK_DIM + half: (h + 1) * RET_QK_DIM]
            out_ref[0, :, h * RET_QK_DIM: h * RET_QK_DIM + half] = (
                (a * cos - b * sin) * scale).astype(BF16)
            out_ref[0, :, h * RET_QK_DIM + half: (h + 1) * RET_QK_DIM] = (
                (a * sin + b * cos) * scale).astype(BF16)

    c0 = 0
    rope_store(proj(c0, RET_QK), q_ref, 1.0)
    c0 += RET_QK
    rope_store(proj(c0, RET_QK), k_ref, RET_QK_DIM ** -0.5)
    c0 += RET_QK
    for j in range(RET_V // 1024):
        v_ref[0, :, j * 1024:(j + 1) * 1024] = proj(c0 + j * 1024, 1024).astype(BF16)
    c0 += RET_V
    for j in range(RET_V // 1024):
        g = proj(c0 + j * 1024, 1024)
        gs_ref[0, :, j * 1024:(j + 1) * 1024] = (g * _sigmoid(g)).astype(BF16)
    c0 += RET_V
    cb_ref[0] = proj(c0, D_MODEL).astype(BF16)
    c0 += D_MODEL
    cc = proj(c0, D_MODEL)
    c0 += D_MODEL
    u_ref[0] = cc * proj(c0, D_MODEL)
    c0 += D_MODEL
    sga_ref[0] = _sigmoid(proj(c0, D_MODEL)).astype(BF16)
    c0 += D_MODEL
    sgb_ref[0] = _sigmoid(proj(c0, D_MODEL)).astype(BF16)


def _in_proj(x, mod, g1, cos, sin, w_in_bf16):
    bsz, t, d = x.shape
    tm = min(PROJ_ROWS, t)
    row = lambda width: pl.BlockSpec((1, tm, width), lambda b, i: (b, i, 0))
    shp = lambda width, dt: jax.ShapeDtypeStruct((bsz, t, width), dt)
    return pl.pallas_call(
        _in_proj_kernel,
        grid=(bsz, t // tm),
        in_specs=[row(d),
                  pl.BlockSpec((1, 6, d), lambda b, i: (b, 0, 0)),
                  pl.BlockSpec((1, d), lambda b, i: (0, 0)),
                  pl.BlockSpec((tm, RET_QK_DIM // 2), lambda b, i: (i, 0)),
                  pl.BlockSpec((tm, RET_QK_DIM // 2), lambda b, i: (i, 0)),
                  pl.BlockSpec((d, PROJ_WIDTH), lambda b, i: (0, 0),
                               pipeline_mode=pl.Buffered(1))],
        out_specs=[row(RET_QK), row(RET_QK), row(RET_V), row(RET_V),
                   row(d), row(d), row(d), row(d)],
        out_shape=[shp(RET_QK, BF16), shp(RET_QK, BF16), shp(RET_V, BF16), shp(RET_V, BF16),
                   shp(d, BF16), shp(d, F32), shp(d, BF16), shp(d, BF16)],
        compiler_params=_params("parallel", "parallel"),
        name="in_proj",
    )(x, mod, g1, cos, sin, w_in_bf16)


def _conv_branch(u_ref, cb_ref, cache_ref, cw_ref, w_ref, new_ref, prev_ref):
    i = pl.program_id(1)

    @pl.when(i == 0)
    def _():
        prev_ref[...] = cache_ref[0]

    u = u_ref[0]
    tc = u.shape[0]
    prev = prev_ref[...]
    rows = lax.broadcasted_iota(jnp.int32, u.shape, 0)
    u1 = jnp.where(rows == 0, prev[1:2], pltpu.roll(u, 1, axis=0))
    u2 = jnp.where(rows == 0, prev[0:1], jnp.where(rows == 1, prev[1:2], pltpu.roll(u, 2, axis=0)))
    cw = cw_ref[...]
    conv = u2 * cw[0:1] + u1 * cw[1:2] + u * cw[2:3]
    z = (cb_ref[0].astype(F32) * conv).astype(BF16)
    tail = u_ref[0, tc - 2:tc, :]
    prev_ref[...] = tail
    new_ref[0] = tail
    return jnp.dot(z, w_ref[...], preferred_element_type=F32)


def _decay_tables(block):
    lg = np.log(1.0 - np.exp2(-5.0 - np.arange(RET_HEADS, dtype=np.float64)))
    idx = np.arange(block, dtype=np.float64)
    dist = np.abs(idx[:, None] - idx[None, :])
    visible = (idx[None, :] // CHUNK) <= (idx[:, None] // CHUNK)
    dmat = np.where(visible[None], np.exp(dist[None] * lg[:, None, None]), 0.0)
    qdec = np.exp((idx + 1.0)[None, :] * lg[:, None])
    kdec = np.exp((block - 1.0 - idx)[None, :] * lg[:, None])
    sdec = np.exp(block * lg)
    lanes = lambda a: jnp.asarray(np.broadcast_to(a[:, :, None], (RET_HEADS, block, 128)), F32)
    return jnp.asarray(dmat, F32), lanes(qdec), lanes(kdec), [float(s) for s in sdec]


def _ret_kernel(sdec, q_ref, k_ref, v_ref, gs_ref, u_ref, cb_ref, sga_ref, sgb_ref, x_ref, mod_ref,
                s0_ref, cache_ref, dmat_ref, qdec_ref, kdec_ref, cw_ref, wco_ref, wro_ref, wmo_ref,
                g2_ref, x1_ref, xn2_ref, s_ref, conv_new_ref, conv_prev_ref):
    i = pl.program_id(1)
    yconv = _conv_branch(u_ref, cb_ref, cache_ref, cw_ref, wco_ref, conv_new_ref, conv_prev_ref)

    @pl.when(i == 0)
    def _():
        s_ref[...] = s0_ref[...]

    ls = q_ref.shape[1]
    yret = jnp.zeros((ls, D_MODEL), F32)
    for h in range(RET_HEADS):
        qh = q_ref[0, :, h * RET_QK_DIM:(h + 1) * RET_QK_DIM]
        kh = k_ref[0, :, h * RET_QK_DIM:(h + 1) * RET_QK_DIM]
        vh = v_ref[0, :, h * RET_V_DIM:(h + 1) * RET_V_DIM]
        state = s_ref[0, h]
        scores = lax.dot_general(qh, kh, (((1,), (1,)), ((), ())), preferred_element_type=F32)
        p = (scores * dmat_ref[h]).astype(BF16)
        o = jnp.dot(p, vh, preferred_element_type=F32)
        qdec = jnp.concatenate([qdec_ref[h]] * (RET_V_DIM // 128), axis=1)
        o = o + jnp.dot(qh, state.astype(BF16), preferred_element_type=F32) * qdec
        kdec = jnp.concatenate([kdec_ref[h]] * (RET_QK_DIM // 128), axis=1)
        kd = (kh.astype(F32) * kdec).astype(BF16)
        s_ref[0, h] = sdec[h] * state + lax.dot_general(
            kd, vh, (((0,), (0,)), ((), ())), preferred_element_type=F32)
        og = (gs_ref[0, :, h * RET_V_DIM:(h + 1) * RET_V_DIM].astype(F32) * _rms(o)).astype(BF16)
        yret = yret + jnp.dot(og, wro_ref[h * RET_V_DIM:(h + 1) * RET_V_DIM, :],
                              preferred_element_type=F32)
    hmix = sga_ref[0].astype(F32) * yconv + sgb_ref[0].astype(F32) * yret
    mix = jnp.dot(hmix.astype(BF16), wmo_ref[...], preferred_element_type=F32)
    mod = mod_ref[0]
    x1 = x_ref[0] + mod[2:3] * mix
    x1_ref[0] = x1
    xn2 = _rms(x1) * g2_ref[...]
    xn2_ref[0] = xn2 * (1.0 + mod[4:5]) + mod[3:4]


def _ret(q, k, v, gs, u, cb, sga, sgb, x, mod, s0, conv_cache, conv_w, w_conv_out_bf16,
         w_ret_out_bf16, w_mix_out_bf16, g2):
    bsz, t, d = x.shape
    ls = min(RET_BLOCK, t)
    dmat, qdec, kdec, sdec = _decay_tables(ls)
    row = lambda width: pl.BlockSpec((1, ls, width), lambda b, i: (b, i, 0))
    const = lambda shape: pl.BlockSpec(shape, lambda b, i: (0,) * len(shape))
    state_spec = pl.BlockSpec((1, RET_HEADS, RET_QK_DIM, RET_V_DIM), lambda b, i: (b, 0, 0, 0))
    cache_spec = pl.BlockSpec((1, CONV_K - 1, d), lambda b, i: (b, 0, 0))
    return pl.pallas_call(
        functools.partial(_ret_kernel, sdec),
        grid=(bsz, t // ls),
        in_specs=[row(RET_QK), row(RET_QK), row(RET_V), row(RET_V), row(d), row(d), row(d), row(d),
                  row(d),
                  pl.BlockSpec((1, 6, d), lambda b, i: (b, 0, 0)),
                  state_spec, cache_spec,
                  const((RET_HEADS, ls, ls)), const((RET_HEADS, ls, 128)),
                  const((RET_HEADS, ls, 128)),
                  const((CONV_K, d)), const((d, d)),
                  const((RET_V, d)), const((d, d)), const((1, d))],
        out_specs=[row(d), row(d), state_spec, cache_spec],
        out_shape=[jax.ShapeDtypeStruct((bsz, t, d), F32),
                   jax.ShapeDtypeStruct((bsz, t, d), F32),
                   jax.ShapeDtypeStruct(s0.shape, F32),
                   jax.ShapeDtypeStruct((bsz, CONV_K - 1, d), F32)],
        scratch_shapes=[pltpu.VMEM((CONV_K - 1, d), F32)],
        compiler_params=_params("parallel", "arbitrary"),
        name="ret",
    )(q, k, v, gs, u, cb, sga, sgb, x, mod, s0, conv_cache, dmat, qdec, kdec, conv_w,
      w_conv_out_bf16, w_ret_out_bf16, w_mix_out_bf16, g2)


def _two_stream_specs(n_p, n_s, tm, width):
    tiles_p = n_p // tm
    last_p = tiles_p - 1
    return (pl.BlockSpec((tm, width), lambda i: (jnp.minimum(i, last_p), 0)),
            pl.BlockSpec((tm, width), lambda i: (jnp.maximum(i - tiles_p, 0), 0)))


def _router_kernel(tiles_p, xp_ref, xs_ref, wt_ref, b_ref, idx_ref, prob_ref, rank_ref, base_ref,
                   cnt_ref, run_ref):
    i = pl.program_id(0)

    @pl.when(i == 0)
    def _():
        run_ref[...] = jnp.zeros_like(run_ref)

    diag = (lax.broadcasted_iota(jnp.int32, (N_EXPERTS, N_EXPERTS), 0)
            == lax.broadcasted_iota(jnp.int32, (N_EXPERTS, N_EXPERTS), 1))
    as_row = lambda column: jnp.sum(jnp.where(diag, column, 0.0), axis=0, keepdims=True)
    base_ref[0] = as_row(run_ref[...]).astype(jnp.int32)

    xn = jnp.where(i < tiles_p, xp_ref[...], xs_ref[...]).astype(BF16)
    logits = lax.dot_general(wt_ref[...], xn, (((1,), (1,)), ((), ())),
                             preferred_element_type=F32) + b_ref[...]
    tm = logits.shape[1]
    expert = lax.broadcasted_iota(jnp.int32, logits.shape, 0)
    slot = lax.broadcasted_iota(jnp.int32, (TOP_K, tm), 0)
    work = logits
    sels, vals = [], []
    idx = jnp.zeros((TOP_K, tm), jnp.int32)
    for k in range(TOP_K):
        m = jnp.max(work, axis=0, keepdims=True)
        first = jnp.min(jnp.where(work == m, expert, N_EXPERTS), axis=0, keepdims=True)
        sel = expert == first
        sels.append(sel)
        vals.append(m)
        idx = jnp.where(slot == k, first, idx)
        work = jnp.where(sel, -jnp.inf, work)
    exps = [jnp.exp(v - vals[0]) for v in vals]
    denom = exps[0] + exps[1] + exps[2] + exps[3]
    prob = jnp.zeros((TOP_K, tm), F32)
    for k in range(TOP_K):
        prob = jnp.where(slot == k, exps[k] / denom, prob)
    member = jnp.zeros(logits.shape, F32)
    for sel in sels:
        member = member + jnp.where(sel, 1.0, 0.0)
    src = lax.broadcasted_iota(jnp.int32, (tm, tm), 0)
    dst = lax.broadcasted_iota(jnp.int32, (tm, tm), 1)
    earlier = jnp.where(src < dst, 1.0, 0.0).astype(BF16)
    before = jnp.dot(member.astype(BF16), earlier, preferred_element_type=F32) + run_ref[...]
    rank = jnp.zeros((TOP_K, tm), jnp.int32)
    for k in range(TOP_K):
        rk = jnp.sum(jnp.where(sels[k], before, 0.0), axis=0, keepdims=True)
        rank = jnp.where(slot == k, rk.astype(jnp.int32), rank)
    run_ref[...] += jnp.sum(member, axis=1, keepdims=True)
    idx_ref[...] = idx
    prob_ref[...] = prob
    rank_ref[...] = rank
    cnt_ref[...] = as_row(run_ref[...]).astype(jnp.int32)


def _router(xn2_p, xn2_s, w_router_t_bf16, b_router):
    (n_p, d), n_s = xn2_p.shape, xn2_s.shape[0]
    tm = TOK_TILE
    n = n_p + n_s
    spec_p, spec_s = _two_stream_specs(n_p, n_s, tm, d)
    pairs = pl.BlockSpec((TOP_K, tm), lambda i: (0, i))
    return pl.pallas_call(
        functools.partial(_router_kernel, n_p // tm),
        grid=(n // tm,),
        in_specs=[spec_p, spec_s,
                  pl.BlockSpec((N_EXPERTS, d), lambda i: (0, 0)),
                  pl.BlockSpec((N_EXPERTS, 1), lambda i: (0, 0))],
        out_specs=[pairs, pairs, pairs,
                   pl.BlockSpec((1, 1, N_EXPERTS), lambda i: (i, 0, 0)),
                   pl.BlockSpec((1, N_EXPERTS), lambda i: (0, 0))],
        out_shape=[jax.ShapeDtypeStruct((TOP_K, n), jnp.int32),
                   jax.ShapeDtypeStruct((TOP_K, n), F32),
                   jax.ShapeDtypeStruct((TOP_K, n), jnp.int32),
                   jax.ShapeDtypeStruct((n // tm, 1, N_EXPERTS), jnp.int32),
                   jax.ShapeDtypeStruct((1, N_EXPERTS), jnp.int32)],
        scratch_shapes=[pltpu.VMEM((N_EXPERTS, 1), F32)],
        compiler_params=_params("arbitrary"),
        name="router",
    )(xn2_p, xn2_s, w_router_t_bf16, b_router.reshape(N_EXPERTS, 1))


def _dispatch_kernel(tiles_p, pos_hbm, xp_ref, xs_ref, out_hbm, pos_smem, row_buf, sem_pos, sem_rows):
    i = pl.program_id(0)
    n = pl.num_programs(0)
    groups, sub, d = row_buf.shape[1:]
    tm = groups * sub
    slot = i % 2

    per_tile = tm * TOP_K

    def pos_copy(step, s):
        return pltpu.make_async_copy(pos_hbm.at[pl.ds(step * per_tile, per_tile)],
                                     pos_smem.at[pl.ds(s * per_tile, per_tile)], sem_pos.at[s])

    def drain(s):
        for _ in range(TOP_K):
            pltpu.make_async_copy(xp_ref, out_hbm.at[pl.ds(0, tm)], sem_rows.at[s]).wait()

    @pl.when(i == 0)
    def _():
        pos_copy(0, 0).start()

    @pl.when(i + 1 < n)
    def _():
        pos_copy(i + 1, 1 - slot).start()

    @pl.when(i >= 2)
    def _():
        drain(slot)

    row_buf[slot] = jnp.where(i < tiles_p, xp_ref[...], xs_ref[...]).reshape(groups, sub, d)
    pos_copy(i, slot).wait()
    pos_base = slot * per_tile

    def issue(g, carry):
        for u in range(sub):
            for k in range(TOP_K):
                dst = pos_smem[pos_base + g * sub + (k * tm + u)]
                pltpu.make_async_copy(row_buf.at[slot, g, pl.ds(u, 1)], out_hbm.at[pl.ds(dst, 1)],
                                      sem_rows.at[slot]).start(priority=k % 2)
        return carry

    lax.fori_loop(0, groups, issue, 0)

    @pl.when(i == n - 1)
    def _():
        drain(slot)

    @pl.when((i == n - 1) & (n >= 2))
    def _():
        drain(1 - slot)


def _dispatch(xn2_p, xn2_s, pos):
    (n_p, d), n_s = xn2_p.shape, xn2_s.shape[0]
    tm = DISPATCH_TILE
    n = n_p + n_s
    spec_p, spec_s = _two_stream_specs(n_p, n_s, tm, d)
    where = pos.reshape(TOP_K, n // tm, tm).transpose(1, 0, 2).reshape(-1)
    return pl.pallas_call(
        functools.partial(_dispatch_kernel, n_p // tm),
        grid=(n // tm,),
        in_specs=[pl.BlockSpec(memory_space=pl.ANY), spec_p, spec_s],
        out_specs=pl.BlockSpec(memory_space=pl.ANY),
        out_shape=jax.ShapeDtypeStruct((n * TOP_K, d), F32),
        scratch_shapes=[pltpu.SMEM((2 * tm * TOP_K,), jnp.int32),
                        pltpu.VMEM((2, tm // F32_SUBLANES, F32_SUBLANES, d), F32),
                        pltpu.SemaphoreType.DMA((2,)),
                        pltpu.SemaphoreType.DMA((2,))],
        compiler_params=_params("arbitrary"),
        name="dispatch",
    )(where, xn2_p, xn2_s)


def _visit_schedule(counts, n_rows, tm):
    n_tiles = n_rows // tm
    n_visits = n_tiles + N_EXPERTS
    off = jnp.concatenate([jnp.zeros((1,), jnp.int32), jnp.cumsum(counts, dtype=jnp.int32)])
    first_tile = off[:-1] // tm
    last_tile = (off[1:] - 1) // tm
    per_expert = jnp.where(counts > 0, last_tile - first_tile + 1, 0)
    ends = jnp.cumsum(per_expert, dtype=jnp.int32)
    total = ends[-1]
    step = jnp.arange(n_visits, dtype=jnp.int32)
    v = jnp.minimum(step, total - 1)
    expert = jnp.sum((ends[None, :] <= v[:, None]).astype(jnp.int32), axis=1)
    mine = expert[:, None] == jnp.arange(N_EXPERTS, dtype=jnp.int32)[None, :]
    pick = lambda table: jnp.sum(jnp.where(mine, table[None, :], 0), axis=1)
    live = step < total
    tile = jnp.where(live, pick(first_tile) + (v - pick(ends - per_expert)), n_tiles)
    lo = jnp.where(live, jnp.clip(pick(off[:-1]) - tile * tm, 0, tm), 0)
    hi = jnp.where(live, jnp.clip(pick(off[1:]) - tile * tm, 0, tm), 0)
    changed = jnp.concatenate([jnp.ones((1,), jnp.bool_), tile[1:] != tile[:-1]])
    mode = jnp.where(live, jnp.where(changed, VISIT_FIRST, VISIT_MERGE),
                     jnp.where(step == total, VISIT_ZERO, VISIT_MERGE)).astype(jnp.int32)
    return tile, expert, lo, hi, mode, off


def _expert_kernel(tile_ref, expert_ref, lo_ref, hi_ref, mode_ref,
                   x_ref, wg_ref, bg_ref, wu_ref, bu_ref, wd_ref, bd_ref, y_ref, w_gu, w_dn):
    v = pl.program_id(0)
    lo = lo_ref[v]
    hi = hi_ref[v]

    @pl.when((v == 0) | (expert_ref[v] != expert_ref[jnp.maximum(v - 1, 0)]))
    def _():
        w_gu[:, 0:D_MODEL] = wg_ref[0].astype(BF16)
        w_gu[:, D_MODEL:2 * D_MODEL] = wu_ref[0].astype(BF16)
        w_dn[...] = wd_ref[0].astype(BF16)

    @pl.when(mode_ref[v] == VISIT_ZERO)
    def _():
        y_ref[...] = jnp.zeros_like(y_ref)

    @pl.when(hi > lo)
    def _():
        xn = x_ref[...].astype(BF16)
        gate_up = jnp.dot(xn, w_gu[...], preferred_element_type=F32)
        gl = jnp.minimum(gate_up[:, 0:D_MODEL] + bg_ref[0], SWIGLU_LIMIT)
        lin = jnp.clip(gate_up[:, D_MODEL:2 * D_MODEL] + bu_ref[0], -SWIGLU_LIMIT, SWIGLU_LIMIT)
        act = gl * _sigmoid(SWIGLU_ALPHA * gl) * (lin + 1.0)
        out = jnp.dot(act.astype(BF16), w_dn[...], preferred_element_type=F32) + bd_ref[0]
        row = lax.broadcasted_iota(jnp.int32, out.shape, 0)
        mine = (row >= lo) & (row < hi)

        @pl.when(mode_ref[v] == VISIT_FIRST)
        def _():
            y_ref[...] = pltpu.bitcast(jnp.where(mine, out, 0.0).astype(BF16), jnp.uint32)

        @pl.when(mode_ref[v] == VISIT_MERGE)
        def _():
            prev = pltpu.bitcast(y_ref[...], BF16)
            y_ref[...] = pltpu.bitcast(jnp.where(mine, out.astype(BF16), prev), jnp.uint32)


def _experts(xs, schedule, wg, bg, wu, bu, wd, bd):
    n_rows, d = xs.shape
    tm = EXPERT_ROWS
    last_in = n_rows // tm - 1
    tile, expert, lo, hi, mode = schedule
    rows_in = pl.BlockSpec((tm, d), lambda v, tile, *_: (jnp.minimum(tile[v], last_in), 0))
    rows_out = pl.BlockSpec((tm // 2, d), lambda v, tile, *_: (tile[v], 0))
    wspec = pl.BlockSpec((1, d, d), lambda v, tile, expert, *_: (expert[v], 0, 0))
    bspec = pl.BlockSpec((1, 1, d), lambda v, tile, expert, *_: (expert[v], 0, 0))
    return pl.pallas_call(
        _expert_kernel,
        grid_spec=pltpu.PrefetchScalarGridSpec(
            num_scalar_prefetch=5,
            grid=(tile.shape[0],),
            in_specs=[rows_in, wspec, bspec, wspec, bspec, wspec, bspec],
            out_specs=rows_out,
            scratch_shapes=[pltpu.VMEM((d, 2 * d), BF16), pltpu.VMEM((d, d), BF16)]),
        out_shape=jax.ShapeDtypeStruct(((n_rows + tm) // 2, d), jnp.uint32),
        compiler_params=_params("arbitrary"),
        name="experts",
    )(tile, expert, lo, hi, mode, xs, wg, bg, wu, bu, wd, bd)


def _combine_plan(pos, idx, off, counts, tile_base, n_rows):
    w, slots = COMBINE_WINDOW, COMBINE_SLOTS
    base = tile_base.reshape(-1, N_EXPERTS)
    n_tiles = base.shape[0]
    seg_start = off[:-1][None, :] + base
    n_seg = jnp.concatenate([base[1:], counts[None, :]], axis=0) - base
    win_start = (seg_start // BF16_SUBLANES) * BF16_SUBLANES
    n_win = jnp.where(n_seg > 0, (seg_start - win_start + n_seg + w - 1) // w, 0)
    slot_end = jnp.cumsum(n_win, axis=1, dtype=jnp.int32)
    slot_base = slot_end - n_win
    experts = jnp.arange(N_EXPERTS, dtype=jnp.int32)
    per_token = lambda table: jnp.repeat(table.T, TOK_TILE, axis=1)[:, None, :]
    pick = lambda table: jnp.sum(
        jnp.where(idx[None] == experts[:, None, None], per_token(table), 0), axis=0)
    loc = pos - pick(win_start)
    slot = pick(slot_base) + loc // w
    col = jnp.where(slot < slots, slot * w + loc % w, -1)
    q = jnp.arange(slots, dtype=jnp.int32)[None, :, None]
    e_q = jnp.sum((slot_end[:, None, :] <= q).astype(jnp.int32), axis=-1)
    of_slot = lambda table: jnp.sum(jnp.where(e_q[..., None] == experts, table[:, None, :], 0), -1)
    used = q[..., 0] < slot_end[:, -1:]
    slot_start = jnp.where(used, of_slot(win_start) + (q[..., 0] - of_slot(slot_base)) * w, n_rows)
    fb_pass = jnp.where(col < 0, loc // w + 1, 0).reshape(TOP_K, n_tiles, TOK_TILE)
    return col, slot_start, win_start, loc, jnp.max(fb_pass, axis=(0, 2))


def _combine_kernel(tiles_p, n_p, t_p, b_p, t_s, slot_ref, fb_ref, npass_ref, ys_hbm, col_ref,
                    idx_ref, loc_ref, prob_ref, x1p_ref, x1s_ref, mod_ref, gf_ref, yp_ref, ysm_ref,
                    win_buf, sem):
    i = pl.program_id(0)
    n = pl.num_programs(0)
    tm = idx_ref.shape[0]
    slot = i % 2
    w_rows = COMBINE_WINDOW
    w_words = w_rows // 2

    def fetch(starts_ref, count, first, shift, s):
        for j in range(count):
            start = pl.multiple_of(starts_ref[first + j] + shift, F32_SUBLANES)
            pltpu.make_async_copy(ys_hbm.at[pl.ds(start, w_words)],
                                  win_buf.at[s, pl.ds(j * w_words, w_words)],
                                  sem.at[s]).start(priority=j % 2)

    def wait(count, s):
        pltpu.make_async_copy(ys_hbm.at[pl.ds(0, count * w_words)],
                              win_buf.at[s, pl.ds(0, count * w_words)], sem.at[s]).wait()

    def mix(cols, count, s):
        lane = lax.broadcasted_iota(jnp.int32, (tm, count * w_rows), 1)
        prob = prob_ref[...]
        coef = jnp.zeros(lane.shape, F32)
        for k in range(TOP_K):
            coef = jnp.where(lane == cols[:, k:k + 1], prob[:, k:k + 1], coef)
        rows = pltpu.bitcast(win_buf[s, 0:count * w_words, :], BF16)
        return jnp.dot(coef.astype(BF16), rows, preferred_element_type=F32)

    @pl.when(i == 0)
    def _():
        fetch(slot_ref, COMBINE_SLOTS, 0, 0, 0)

    @pl.when(i + 1 < n)
    def _():
        fetch(slot_ref, COMBINE_SLOTS, (i + 1) * COMBINE_SLOTS, 0, 1 - slot)

    wait(COMBINE_SLOTS, slot)
    covered = col_ref[...]
    moe = mix(covered, COMBINE_SLOTS, slot)

    def fallback_pass(p, acc):
        fetch(fb_ref, N_EXPERTS, i * N_EXPERTS, p * w_words, slot)
        wait(N_EXPERTS, slot)
        lw = loc_ref[...] - p * w_rows
        cols = jnp.where((covered < 0) & (lw >= 0) & (lw < w_rows), idx_ref[...] * w_rows + lw, -1)
        return acc + mix(cols, N_EXPERTS, slot)

    moe = lax.fori_loop(0, npass_ref[i], fallback_pass, moe)
    x1 = jnp.where(i < tiles_p, x1p_ref[...], x1s_ref[...])
    gate_rows = []
    for j in range(tm // CHUNK):
        n0 = i * tm + j * CHUNK
        b = jnp.where(n0 < n_p, n0 // t_p, b_p + (n0 - n_p) // t_s)
        gate_rows.append(jnp.broadcast_to(mod_ref[b, 5:6, :], (CHUNK, D_MODEL)))
    x2 = x1 + jnp.concatenate(gate_rows, axis=0) * moe
    y = _rms(x2) * gf_ref[...]

    @pl.when(i < tiles_p)
    def _():
        yp_ref[...] = y

    @pl.when(i >= tiles_p)
    def _():
        ysm_ref[...] = y


def _combine(ys, plan, idx, prob, x1_p, x1_s, mod, t_p, t_s, g_final):
    col, slot_start, fb_start, loc, fb_passes = plan
    col, idx, loc, prob = col.T, idx.T, loc.T, prob.T
    (n_p, d), n_s = x1_p.shape, x1_s.shape[0]
    tm = TOK_TILE
    n = n_p + n_s
    assert t_p % CHUNK == 0 and t_s % CHUNK == 0 and tm % CHUNK == 0
    tiles_p = n_p // tm
    last_p = tiles_p - 1
    spec_p = pl.BlockSpec((tm, d), lambda i, *_: (jnp.minimum(i, last_p), 0))
    spec_s = pl.BlockSpec((tm, d), lambda i, *_: (jnp.maximum(i - tiles_p, 0), 0))
    tok4 = pl.BlockSpec((tm, TOP_K), lambda i, *_: (i, 0))
    return pl.pallas_call(
        functools.partial(_combine_kernel, tiles_p, n_p, t_p, n_p // t_p, t_s),
        grid_spec=pltpu.PrefetchScalarGridSpec(
            num_scalar_prefetch=3,
            grid=(n // tm,),
            in_specs=[pl.BlockSpec(memory_space=pl.ANY), tok4, tok4, tok4, tok4, spec_p, spec_s,
                      pl.BlockSpec(mod.shape, lambda i, *_: (0, 0, 0)),
                      pl.BlockSpec((1, d), lambda i, *_: (0, 0))],
            out_specs=[spec_p, spec_s],
            scratch_shapes=[pltpu.VMEM((2, max(COMBINE_SLOTS, N_EXPERTS) * COMBINE_WINDOW // 2, d),
                                       jnp.uint32),
                            pltpu.SemaphoreType.DMA((2,))]),
        out_shape=[jax.ShapeDtypeStruct((n_p, d), F32), jax.ShapeDtypeStruct((n_s, d), F32)],
        compiler_params=_params("arbitrary"),
        name="combine",
    )((slot_start // 2).reshape(-1), (fb_start // 2).reshape(-1), fb_passes,
      ys, col, idx, loc, prob, x1_p, x1_s, mod, g_final)


def _rope_tables(t, pos0):
    half = RET_QK_DIM // 2
    inv = ROPE_BASE ** (-jnp.arange(0, RET_QK_DIM, 2, dtype=F32) / RET_QK_DIM)
    pos = jnp.arange(t, dtype=F32) + float(pos0)
    ang = pos[:, None] * inv[None, :]
    assert ang.shape == (t, half)
    return jnp.cos(ang), jnp.sin(ang)


def _mixer(x, mod, pos0, conv_cache, s0, wts):
    bsz, t, d = x.shape
    cos, sin = _rope_tables(t, pos0)
    q, k, v, gs, cb, u, sga, sgb = _in_proj(x, mod, wts["g1"], cos, sin, wts["w_in"])
    x1, xn2, s_new, conv_new = _ret(q, k, v, gs, u, cb, sga, sgb, x, mod, s0, conv_cache,
                                    wts["conv_w"], wts["w_conv_out"], wts["w_ret_out"],
                                    wts["w_mix_out"], wts["g2"])
    return x1.reshape(bsz * t, d), xn2.reshape(bsz * t, d), conv_new[None], s_new[None]


def _moe(x1_p, xn2_p, x1_s, xn2_s, mod, t_p, t_s, wts):
    n = x1_p.shape[0] + x1_s.shape[0]
    idx, prob, rank, tile_base, counts = _router(xn2_p, xn2_s, wts["w_router"], wts["b_router"])
    counts = counts.reshape(N_EXPERTS)
    *schedule, off = _visit_schedule(counts, n * TOP_K, EXPERT_ROWS)
    experts = jnp.arange(N_EXPERTS, dtype=jnp.int32)[:, None, None]
    pos = jnp.sum(jnp.where(idx[None] == experts, off[:-1, None, None], 0), axis=0) + rank
    xs = _dispatch(xn2_p, xn2_s, pos)
    ys = _experts(xs, schedule, wts["w_gate"], wts["b_gate"], wts["w_up"], wts["b_up"],
                  wts["w_down"], wts["b_down"])
    plan = _combine_plan(pos, idx, off, counts, tile_base, n * TOP_K)
    return _combine(ys, plan, idx, prob, x1_p, x1_s, mod, t_p, t_s, wts["g_final"])


def kernel(x_prompt, x_sample, c_prompt, c_sample, cache_conv, state_ret, w_ada, b_ada, g_norm1,
           w_in, conv_w, w_ret_out, w_conv_out, w_mix_out, g_norm2, w_router, b_router,
           w_gate, b_gate, w_up, b_up, w_down, b_down, g_final):
    assert w_ada.shape[0] == 1, "single-layer model"
    d = D_MODEL
    bp = x_prompt.shape[0]
    bs = x_sample.shape[0]
    wts = {
        "g1": g_norm1[0].reshape(1, d),
        "w_in": w_in[0].astype(BF16),
        "conv_w": conv_w[0],
        "w_ret_out": w_ret_out[0].astype(BF16),
        "w_conv_out": w_conv_out[0].astype(BF16),
        "w_mix_out": w_mix_out[0].astype(BF16),
        "g2": g_norm2[0].reshape(1, d),
        "w_router": w_router[0].T.astype(BF16),
        "b_router": b_router[0],
        "w_gate": w_gate[0],
        "b_gate": b_gate[0].reshape(N_EXPERTS, 1, d),
        "w_up": w_up[0],
        "b_up": b_up[0].reshape(N_EXPERTS, 1, d),
        "w_down": w_down[0],
        "b_down": b_down[0].reshape(N_EXPERTS, 1, d),
        "g_final": g_final.reshape(1, d),
    }
    mod = _ada(jnp.concatenate([c_prompt, c_sample], axis=0), w_ada[0], b_ada[0])
    mod = mod.reshape(bp + bs, 6, d)
    conv0 = jnp.zeros((bp, CONV_K - 1, d), F32)
    ret0 = jnp.zeros((bp, RET_HEADS, RET_QK_DIM, RET_V_DIM), F32)
    t_p, t_s = x_prompt.shape[1], x_sample.shape[1]
    x1_p, xn2_p, conv_p, ret_p = _mixer(x_prompt, mod[:bp], 0, conv0, ret0, wts)
    x1_s, xn2_s, conv_s, ret_s = _mixer(x_sample, mod[bp:], PAST_LEN, cache_conv[0], state_ret[0],
                                        wts)
    y_p, y_s = _moe(x1_p, xn2_p, x1_s, xn2_s, mod, t_p, t_s, wts)
    return (y_p.reshape(x_prompt.shape), y_s.reshape(x_sample.shape), conv_p, ret_p, conv_s, ret_s)
```

```python
import functools

import jax
import jax.numpy as jnp
import numpy as np
from jax import lax
from jax.experimental import pallas as pl
from jax.experimental.pallas import tpu as pltpu

D_MODEL = 1024
CHUNK = 64
RET_HEADS = 4
RET_QK_DIM = 256
RET_V_DIM = 512
RET_QK = RET_HEADS * RET_QK_DIM
RET_V = RET_HEADS * RET_V_DIM
CONV_K = 3
N_EXPERTS = 32
TOP_K = 4
SWIGLU_LIMIT = 7.0
SWIGLU_ALPHA = 1.702
ROPE_BASE = 10000.0
PAST_LEN = 4096
EPS = 1e-6
PROJ_WIDTH = 2 * RET_QK + 2 * RET_V + 5 * D_MODEL

V7X_VMEM_LIMIT_BYTES = 56 * 1024 * 1024
RET_BLOCK = 256
PROJ_ROWS = 256
TOK_TILE = 256
DISPATCH_TILE = 512
EXPERT_ROWS = 512
COMBINE_WINDOW = 32
COMBINE_SLOTS = 64
BF16_SUBLANES = 16
F32_SUBLANES = 8
VISIT_FIRST, VISIT_MERGE, VISIT_ZERO = 1, 0, 2

F32 = jnp.float32
BF16 = jnp.bfloat16


def _params(*sem):
    return pltpu.CompilerParams(dimension_semantics=sem, vmem_limit_bytes=V7X_VMEM_LIMIT_BYTES)


def _sigmoid(x):
    return 1.0 / (1.0 + jnp.exp(-x))


def _rms(x):
    return x * lax.rsqrt(jnp.mean(x * x, axis=-1, keepdims=True) + EPS)


def _ada_kernel(c_ref, w_ref, b_ref, o_ref):
    c = c_ref[...]
    s = c * _sigmoid(c)
    o_ref[...] = jnp.dot(s, w_ref[...], preferred_element_type=F32,
                         precision=lax.Precision.HIGHEST) + b_ref[...]


def _ada(c, w, b):
    n, d = c.shape
    width = w.shape[1]
    tn = 1024
    return pl.pallas_call(
        _ada_kernel,
        grid=(width // tn,),
        in_specs=[pl.BlockSpec((n, d), lambda j: (0, 0)),
                  pl.BlockSpec((d, tn), lambda j: (0, j)),
                  pl.BlockSpec((1, tn), lambda j: (0, j))],
        out_specs=pl.BlockSpec((n, tn), lambda j: (0, j)),
        out_shape=jax.ShapeDtypeStruct((n, width), F32),
        compiler_params=_params("arbitrary"),
        name="ada",
    )(c, w, b.reshape(1, width))


def _in_proj_kernel(x_ref, mod_ref, g1_ref, cos_ref, sin_ref, w_ref,
                    q_ref, k_ref, v_ref, gs_ref, cb_ref, u_ref, sga_ref, sgb_ref):
    x = x_ref[0]
    mod = mod_ref[0]
    xn = _rms(x) * g1_ref[...]
    xn = (xn * (1.0 + mod[1:2]) + mod[0:1]).astype(BF16)
    cos = cos_ref[...]
    sin = sin_ref[...]

    def proj(c0, width):
        return jnp.dot(xn, w_ref[:, c0:c0 + width], preferred_element_type=F32)

    def rope_store(acc, out_ref, scale):
        half = RET_QK_DIM // 2
        for h in range(RET_HEADS):
            a = acc[:, h * RET_QK_DIM: h * RET_QK_DIM + half]
            b = acc[:, h * RET_QK_DIM + half: (h + 1) * RET_QK_DIM]
            out_ref[0, :, h * RET_QK_DIM: h * RET_QK_DIM + half] = (
                (a * cos - b * sin) * scale).astype(BF16)
            out_ref[0, :, h * RET_QK_DIM + half: (h + 1) * RET_QK_DIM] = (
                (a * sin + b * cos) * scale).astype(BF16)

    c0 = 0
    rope_store(proj(c0, RET_QK), q_ref, 1.0)
    c0 += RET_QK
    rope_store(proj(c0, RET_QK), k_ref, RET_QK_DIM ** -0.5)
    c0 += RET_QK
    for j in range(RET_V // 1024):
        v_ref[0, :, j * 1024:(j + 1) * 1024] = proj(c0 + j * 1024, 1024).astype(BF16)
    c0 += RET_V
    for j in range(RET_V // 1024):
        g = proj(c0 + j * 1024, 1024)
        gs_ref[0, :, j * 1024:(j + 1) * 1024] = (g * _sigmoid(g)).astype(BF16)
    c0 += RET_V
    cb_ref[0] = proj(c0, D_MODEL).astype(BF16)
    c0 += D_MODEL
    cc = proj(c0, D_MODEL)
    c0 += D_MODEL
    u_ref[0] = cc * proj(c0, D_MODEL)
    c0 += D_MODEL
    sga_ref[0] = _sigmoid(proj(c0, D_MODEL)).astype(BF16)
    c0 += D_MODEL
    sgb_ref[0] = _sigmoid(proj(c0, D_MODEL)).astype(BF16)


def _in_proj(x, mod, g1, cos, sin, w_in_bf16):
    bsz, t, d = x.shape
    tm = min(PROJ_ROWS, t)
    row = lambda width: pl.BlockSpec((1, tm, width), lambda b, i: (b, i, 0))
    shp = lambda width, dt: jax.ShapeDtypeStruct((bsz, t, width), dt)
    return pl.pallas_call(
        _in_proj_kernel,
        grid=(bsz, t // tm),
        in_specs=[row(d),
                  pl.BlockSpec((1, 6, d), lambda b, i: (b, 0, 0)),
                  pl.BlockSpec((1, d), lambda b, i: (0, 0)),
                  pl.BlockSpec((tm, RET_QK_DIM // 2), lambda b, i: (i, 0)),
                  pl.BlockSpec((tm, RET_QK_DIM // 2), lambda b, i: (i, 0)),
                  pl.BlockSpec((d, PROJ_WIDTH), lambda b, i: (0, 0),
                               pipeline_mode=pl.Buffered(1))],
        out_specs=[row(RET_QK), row(RET_QK), row(RET_V), row(RET_V),
                   row(d), row(d), row(d), row(d)],
        out_shape=[shp(RET_QK, BF16), shp(RET_QK, BF16), shp(RET_V, BF16), shp(RET_V, BF16),
                   shp(d, BF16), shp(d, F32), shp(d, BF16), shp(d, BF16)],
        compiler_params=_params("parallel", "parallel"),
        name="in_proj",
    )(x, mod, g1, cos, sin, w_in_bf16)


def _conv_branch(u_ref, cb_ref, cache_ref, cw_ref, w_ref, new_ref, prev_ref):
    i = pl.program_id(1)

    @pl.when(i == 0)
    def _():
        prev_ref[...] = cache_ref[0]

    u = u_ref[0]
    tc = u.shape[0]
    prev = prev_ref[...]
    rows = lax.broadcasted_iota(jnp.int32, u.shape, 0)
    u1 = jnp.where(rows == 0, prev[1:2], pltpu.roll(u, 1, axis=0))
    u2 = jnp.where(rows == 0, prev[0:1], jnp.where(rows == 1, prev[1:2], pltpu.roll(u, 2, axis=0)))
    cw = cw_ref[...]
    conv = u2 * cw[0:1] + u1 * cw[1:2] + u * cw[2:3]
    z = (cb_ref[0].astype(F32) * conv).astype(BF16)
    tail = u_ref[0, tc - 2:tc, :]
    prev_ref[...] = tail
    new_ref[0] = tail
    return jnp.dot(z, w_ref[...], preferred_element_type=F32)


def _decay_tables(block):
    lg = np.log(1.0 - np.exp2(-5.0 - np.arange(RET_HEADS, dtype=np.float64)))
    idx = np.arange(block, dtype=np.float64)
    dist = np.abs(idx[:, None] - idx[None, :])
    visible = (idx[None, :] // CHUNK) <= (idx[:, None] // CHUNK)
    dmat = np.where(visible[None], np.exp(dist[None] * lg[:, None, None]), 0.0)
    qdec = np.exp((idx + 1.0)[None, :] * lg[:, None])
    kdec = np.exp((block - 1.0 - idx)[None, :] * lg[:, None])
    sdec = np.exp(block * lg)
    lanes = lambda a: jnp.asarray(np.broadcast_to(a[:, :, None], (RET_HEADS, block, 128)), F32)
    return jnp.asarray(dmat, F32), lanes(qdec), lanes(kdec), [float(s) for s in sdec]


def _ret_kernel(sdec, q_ref, k_ref, v_ref, gs_ref, u_ref, cb_ref, sga_ref, sgb_ref, x_ref, mod_ref,
                s0_ref, cache_ref, dmat_ref, qdec_ref, kdec_ref, cw_ref, wco_ref, wro_ref, wmo_ref,
                g2_ref, x1_ref, xn2_ref, s_ref, conv_new_ref, conv_prev_ref):
    i = pl.program_id(1)
    yconv = _conv_branch(u_ref, cb_ref, cache_ref, cw_ref, wco_ref, conv_new_ref, conv_prev_ref)

    @pl.when(i == 0)
    def _():
        s_ref[...] = s0_ref[...]

    ls = q_ref.shape[1]
    gated = []
    for h in range(RET_HEADS):
        qh = q_ref[0, :, h * RET_QK_DIM:(h + 1) * RET_QK_DIM]
        kh = k_ref[0, :, h * RET_QK_DIM:(h + 1) * RET_QK_DIM]
        vh = v_ref[0, :, h * RET_V_DIM:(h + 1) * RET_V_DIM]
        state = s_ref[0, h]
        scores = lax.dot_general(qh, kh, (((1,), (1,)), ((), ())), preferred_element_type=F32)
        p = (scores * dmat_ref[h]).astype(BF16)
        o = jnp.dot(p, vh, preferred_element_type=F32)
        qdec = jnp.concatenate([qdec_ref[h]] * (RET_V_DIM // 128), axis=1)
        o = o + jnp.dot(qh, state.astype(BF16), preferred_element_type=F32) * qdec
        kdec = jnp.concatenate([kdec_ref[h]] * (RET_QK_DIM // 128), axis=1)
        kd = (kh.astype(F32) * kdec).astype(BF16)
        s_ref[0, h] = sdec[h] * state + lax.dot_general(
            kd, vh, (((0,), (0,)), ((), ())), preferred_element_type=F32)
        og = (gs_ref[0, :, h * RET_V_DIM:(h + 1) * RET_V_DIM].astype(F32) * _rms(o)).astype(BF16)
        gated.append(og)
    yret = jnp.dot(jnp.concatenate(gated, axis=1), wro_ref[...], preferred_element_type=F32)
    hmix = sga_ref[0].astype(F32) * yconv + sgb_ref[0].astype(F32) * yret
    mix = jnp.dot(hmix.astype(BF16), wmo_ref[...], preferred_element_type=F32)
    mod = mod_ref[0]
    x1 = x_ref[0] + mod[2:3] * mix
    x1_ref[0] = x1
    xn2 = _rms(x1) * g2_ref[...]
    xn2_ref[0] = xn2 * (1.0 + mod[4:5]) + mod[3:4]


def _ret(q, k, v, gs, u, cb, sga, sgb, x, mod, s0, conv_cache, conv_w, w_conv_out_bf16,
         w_ret_out_bf16, w_mix_out_bf16, g2):
    bsz, t, d = x.shape
    ls = min(RET_BLOCK, t)
    dmat, qdec, kdec, sdec = _decay_tables(ls)
    row = lambda width: pl.BlockSpec((1, ls, width), lambda b, i: (b, i, 0))
    const = lambda shape: pl.BlockSpec(shape, lambda b, i: (0,) * len(shape))
    state_spec = pl.BlockSpec((1, RET_HEADS, RET_QK_DIM, RET_V_DIM), lambda b, i: (b, 0, 0, 0))
    cache_spec = pl.BlockSpec((1, CONV_K - 1, d), lambda b, i: (b, 0, 0))
    return pl.pallas_call(
        functools.partial(_ret_kernel, sdec),
        grid=(bsz, t // ls),
        in_specs=[row(RET_QK), row(RET_QK), row(RET_V), row(RET_V), row(d), row(d), row(d), row(d),
                  row(d),
                  pl.BlockSpec((1, 6, d), lambda b, i: (b, 0, 0)),
                  state_spec, cache_spec,
                  const((RET_HEADS, ls, ls)), const((RET_HEADS, ls, 128)),
                  const((RET_HEADS, ls, 128)),
                  const((CONV_K, d)), const((d, d)),
                  const((RET_V, d)), const((d, d)), const((1, d))],
        out_specs=[row(d), row(d), state_spec, cache_spec],
        out_shape=[jax.ShapeDtypeStruct((bsz, t, d), F32),
                   jax.ShapeDtypeStruct((bsz, t, d), F32),
                   jax.ShapeDtypeStruct(s0.shape, F32),
                   jax.ShapeDtypeStruct((bsz, CONV_K - 1, d), F32)],
        scratch_shapes=[pltpu.VMEM((CONV_K - 1, d), F32)],
        compiler_params=_params("parallel", "arbitrary"),
        name="ret",
    )(q, k, v, gs, u, cb, sga, sgb, x, mod, s0, conv_cache, dmat, qdec, kdec, conv_w,
      w_conv_out_bf16, w_ret_out_bf16, w_mix_out_bf16, g2)


def _two_stream_specs(n_p, n_s, tm, width):
    tiles_p = n_p // tm
    last_p = tiles_p - 1
    return (pl.BlockSpec((tm, width), lambda i: (jnp.minimum(i, last_p), 0)),
            pl.BlockSpec((tm, width), lambda i: (jnp.maximum(i - tiles_p, 0), 0)))


def _router_kernel(tiles_p, xp_ref, xs_ref, wt_ref, b_ref, idx_ref, prob_ref, rank_ref, base_ref,
                   cnt_ref, run_ref):
    i = pl.program_id(0)

    @pl.when(i == 0)
    def _():
        run_ref[...] = jnp.zeros_like(run_ref)

    diag = (lax.broadcasted_iota(jnp.int32, (N_EXPERTS, N_EXPERTS), 0)
            == lax.broadcasted_iota(jnp.int32, (N_EXPERTS, N_EXPERTS), 1))
    as_row = lambda column: jnp.sum(jnp.where(diag, column, 0.0), axis=0, keepdims=True)
    base_ref[0] = as_row(run_ref[...]).astype(jnp.int32)

    xn = jnp.where(i < tiles_p, xp_ref[...], xs_ref[...]).astype(BF16)
    logits = lax.dot_general(wt_ref[...], xn, (((1,), (1,)), ((), ())),
                             preferred_element_type=F32) + b_ref[...]
    tm = logits.shape[1]
    expert = lax.broadcasted_iota(jnp.int32, logits.shape, 0)
    slot = lax.broadcasted_iota(jnp.int32, (TOP_K, tm), 0)
    work = logits
    sels, vals = [], []
    idx = jnp.zeros((TOP_K, tm), jnp.int32)
    for k in range(TOP_K):
        m = jnp.max(work, axis=0, keepdims=True)
        first = jnp.min(jnp.where(work == m, expert, N_EXPERTS), axis=0, keepdims=True)
        sel = expert == first
        sels.append(sel)
        vals.append(m)
        idx = jnp.where(slot == k, first, idx)
        work = jnp.where(sel, -jnp.inf, work)
    exps = [jnp.exp(v - vals[0]) for v in vals]
    denom = exps[0] + exps[1] + exps[2] + exps[3]
    prob = jnp.zeros((TOP_K, tm), F32)
    for k in range(TOP_K):
        prob = jnp.where(slot == k, exps[k] / denom, prob)
    member = jnp.zeros(logits.shape, F32)
    for sel in sels:
        member = member + jnp.where(sel, 1.0, 0.0)
    src = lax.broadcasted_iota(jnp.int32, (tm, tm), 0)
    dst = lax.broadcasted_iota(jnp.int32, (tm, tm), 1)
    earlier = jnp.where(src < dst, 1.0, 0.0).astype(BF16)
    before = jnp.dot(member.astype(BF16), earlier, preferred_element_type=F32) + run_ref[...]
    rank = jnp.zeros((TOP_K, tm), jnp.int32)
    for k in range(TOP_K):
        rk = jnp.sum(jnp.where(sels[k], before, 0.0), axis=0, keepdims=True)
        rank = jnp.where(slot == k, rk.astype(jnp.int32), rank)
    run_ref[...] += jnp.sum(member, axis=1, keepdims=True)
    idx_ref[...] = idx
    prob_ref[...] = prob
    rank_ref[...] = rank
    cnt_ref[...] = as_row(run_ref[...]).astype(jnp.int32)


def _router(xn2_p, xn2_s, w_router_t_bf16, b_router):
    (n_p, d), n_s = xn2_p.shape, xn2_s.shape[0]
    tm = TOK_TILE
    n = n_p + n_s
    spec_p, spec_s = _two_stream_specs(n_p, n_s, tm, d)
    pairs = pl.BlockSpec((TOP_K, tm), lambda i: (0, i))
    return pl.pallas_call(
        functools.partial(_router_kernel, n_p // tm),
        grid=(n // tm,),
        in_specs=[spec_p, spec_s,
                  pl.BlockSpec((N_EXPERTS, d), lambda i: (0, 0)),
                  pl.BlockSpec((N_EXPERTS, 1), lambda i: (0, 0))],
        out_specs=[pairs, pairs, pairs,
                   pl.BlockSpec((1, 1, N_EXPERTS), lambda i: (i, 0, 0)),
                   pl.BlockSpec((1, N_EXPERTS), lambda i: (0, 0))],
        out_shape=[jax.ShapeDtypeStruct((TOP_K, n), jnp.int32),
                   jax.ShapeDtypeStruct((TOP_K, n), F32),
                   jax.ShapeDtypeStruct((TOP_K, n), jnp.int32),
                   jax.ShapeDtypeStruct((n // tm, 1, N_EXPERTS), jnp.int32),
                   jax.ShapeDtypeStruct((1, N_EXPERTS), jnp.int32)],
        scratch_shapes=[pltpu.VMEM((N_EXPERTS, 1), F32)],
        compiler_params=_params("arbitrary"),
        name="router",
    )(xn2_p, xn2_s, w_router_t_bf16, b_router.reshape(N_EXPERTS, 1))


def _dispatch_kernel(tiles_p, pos_hbm, xp_ref, xs_ref, out_hbm, pos_smem, row_buf, sem_pos, sem_rows):
    i = pl.program_id(0)
    n = pl.num_programs(0)
    groups, sub, d = row_buf.shape[1:]
    tm = groups * sub
    slot = i % 2

    per_tile = tm * TOP_K

    def pos_copy(step, s):
        return pltpu.make_async_copy(pos_hbm.at[pl.ds(step * per_tile, per_tile)],
                                     pos_smem.at[pl.ds(s * per_tile, per_tile)], sem_pos.at[s])

    def drain(s):
        for _ in range(TOP_K):
            pltpu.make_async_copy(xp_ref, out_hbm.at[pl.ds(0, tm)], sem_rows.at[s]).wait()

    @pl.when(i == 0)
    def _():
        pos_copy(0, 0).start()

    @pl.when(i + 1 < n)
    def _():
        pos_copy(i + 1, 1 - slot).start()

    @pl.when(i >= 2)
    def _():
        drain(slot)

    row_buf[slot] = jnp.where(i < tiles_p, xp_ref[...], xs_ref[...]).reshape(groups, sub, d)
    pos_copy(i, slot).wait()
    pos_base = slot * per_tile

    def issue(g, carry):
        for u in range(sub):
            for k in range(TOP_K):
                dst = pos_smem[pos_base + g * sub + (k * tm + u)]
                pltpu.make_async_copy(row_buf.at[slot, g, pl.ds(u, 1)], out_hbm.at[pl.ds(dst, 1)],
                                      sem_rows.at[slot]).start(priority=k % 2)
        return carry

    lax.fori_loop(0, groups, issue, 0)

    @pl.when(i == n - 1)
    def _():
        drain(slot)

    @pl.when((i == n - 1) & (n >= 2))
    def _():
        drain(1 - slot)


def _dispatch(xn2_p, xn2_s, pos):
    (n_p, d), n_s = xn2_p.shape, xn2_s.shape[0]
    tm = DISPATCH_TILE
    n = n_p + n_s
    spec_p, spec_s = _two_stream_specs(n_p, n_s, tm, d)
    where = pos.reshape(TOP_K, n // tm, tm).transpose(1, 0, 2).reshape(-1)
    return pl.pallas_call(
        functools.partial(_dispatch_kernel, n_p // tm),
        grid=(n // tm,),
        in_specs=[pl.BlockSpec(memory_space=pl.ANY), spec_p, spec_s],
        out_specs=pl.BlockSpec(memory_space=pl.ANY),
        out_shape=jax.ShapeDtypeStruct((n * TOP_K, d), F32),
        scratch_shapes=[pltpu.SMEM((2 * tm * TOP_K,), jnp.int32),
                        pltpu.VMEM((2, tm // F32_SUBLANES, F32_SUBLANES, d), F32),
                        pltpu.SemaphoreType.DMA((2,)),
                        pltpu.SemaphoreType.DMA((2,))],
        compiler_params=_params("arbitrary"),
        name="dispatch",
    )(where, xn2_p, xn2_s)


def _visit_schedule(counts, n_rows, tm):
    n_tiles = n_rows // tm
    n_visits = n_tiles + N_EXPERTS
    off = jnp.concatenate([jnp.zeros((1,), jnp.int32), jnp.cumsum(counts, dtype=jnp.int32)])
    first_tile = off[:-1] // tm
    last_tile = (off[1:] - 1) // tm
    per_expert = jnp.where(counts > 0, last_tile - first_tile + 1, 0)
    ends = jnp.cumsum(per_expert, dtype=jnp.int32)
    total = ends[-1]
    step = jnp.arange(n_visits, dtype=jnp.int32)
    v = jnp.minimum(step, total - 1)
    expert = jnp.sum((ends[None, :] <= v[:, None]).astype(jnp.int32), axis=1)
    mine = expert[:, None] == jnp.arange(N_EXPERTS, dtype=jnp.int32)[None, :]
    pick = lambda table: jnp.sum(jnp.where(mine, table[None, :], 0), axis=1)
    live = step < total
    tile = jnp.where(live, pick(first_tile) + (v - pick(ends - per_expert)), n_tiles)
    lo = jnp.where(live, jnp.clip(pick(off[:-1]) - tile * tm, 0, tm), 0)
    hi = jnp.where(live, jnp.clip(pick(off[1:]) - tile * tm, 0, tm), 0)
    changed = jnp.concatenate([jnp.ones((1,), jnp.bool_), tile[1:] != tile[:-1]])
    mode = jnp.where(live, jnp.where(changed, VISIT_FIRST, VISIT_MERGE),
                     jnp.where(step == total, VISIT_ZERO, VISIT_MERGE)).astype(jnp.int32)
    return tile, expert, lo, hi, mode, off


def _expert_kernel(tile_ref, expert_ref, lo_ref, hi_ref, mode_ref,
                   x_ref, wg_ref, bg_ref, wu_ref, bu_ref, wd_ref, bd_ref, y_ref, w_gu, w_dn):
    v = pl.program_id(0)
    lo = lo_ref[v]
    hi = hi_ref[v]

    @pl.when((v == 0) | (expert_ref[v] != expert_ref[jnp.maximum(v - 1, 0)]))
    def _():
        w_gu[:, 0:D_MODEL] = wg_ref[0].astype(BF16)
        w_gu[:, D_MODEL:2 * D_MODEL] = wu_ref[0].astype(BF16)
        w_dn[...] = wd_ref[0].astype(BF16)

    @pl.when(mode_ref[v] == VISIT_ZERO)
    def _():
        y_ref[...] = jnp.zeros_like(y_ref)

    @pl.when(hi > lo)
    def _():
        xn = x_ref[...].astype(BF16)
        gate_up = jnp.dot(xn, w_gu[...], preferred_element_type=F32)
        gl = jnp.minimum(gate_up[:, 0:D_MODEL] + bg_ref[0], SWIGLU_LIMIT)
        lin = jnp.clip(gate_up[:, D_MODEL:2 * D_MODEL] + bu_ref[0], -SWIGLU_LIMIT, SWIGLU_LIMIT)
        act = gl * _sigmoid(SWIGLU_ALPHA * gl) * (lin + 1.0)
        out = jnp.dot(act.astype(BF16), w_dn[...], preferred_element_type=F32) + bd_ref[0]
        row = lax.broadcasted_iota(jnp.int32, out.shape, 0)
        mine = (row >= lo) & (row < hi)

        @pl.when(mode_ref[v] == VISIT_FIRST)
        def _():
            y_ref[...] = pltpu.bitcast(jnp.where(mine, out, 0.0).astype(BF16), jnp.uint32)

        @pl.when(mode_ref[v] == VISIT_MERGE)
        def _():
            prev = pltpu.bitcast(y_ref[...], BF16)
            y_ref[...] = pltpu.bitcast(jnp.where(mine, out.astype(BF16), prev), jnp.uint32)


def _experts(xs, schedule, wg, bg, wu, bu, wd, bd):
    n_rows, d = xs.shape
    tm = EXPERT_ROWS
    last_in = n_rows // tm - 1
    tile, expert, lo, hi, mode = schedule
    rows_in = pl.BlockSpec((tm, d), lambda v, tile, *_: (jnp.minimum(tile[v], last_in), 0))
    rows_out = pl.BlockSpec((tm // 2, d), lambda v, tile, *_: (tile[v], 0))
    wspec = pl.BlockSpec((1, d, d), lambda v, tile, expert, *_: (expert[v], 0, 0))
    bspec = pl.BlockSpec((1, 1, d), lambda v, tile, expert, *_: (expert[v], 0, 0))
    return pl.pallas_call(
        _expert_kernel,
        grid_spec=pltpu.PrefetchScalarGridSpec(
            num_scalar_prefetch=5,
            grid=(tile.shape[0],),
            in_specs=[rows_in, wspec, bspec, wspec, bspec, wspec, bspec],
            out_specs=rows_out,
            scratch_shapes=[pltpu.VMEM((d, 2 * d), BF16), pltpu.VMEM((d, d), BF16)]),
        out_shape=jax.ShapeDtypeStruct(((n_rows + tm) // 2, d), jnp.uint32),
        compiler_params=_params("arbitrary"),
        name="experts",
    )(tile, expert, lo, hi, mode, xs, wg, bg, wu, bu, wd, bd)


def _combine_plan(pos, idx, off, counts, tile_base, n_rows):
    w, slots = COMBINE_WINDOW, COMBINE_SLOTS
    base = tile_base.reshape(-1, N_EXPERTS)
    n_tiles = base.shape[0]
    seg_start = off[:-1][None, :] + base
    n_seg = jnp.concatenate([base[1:], counts[None, :]], axis=0) - base
    win_start = (seg_start // BF16_SUBLANES) * BF16_SUBLANES
    n_win = jnp.where(n_seg > 0, (seg_start - win_start + n_seg + w - 1) // w, 0)
    slot_end = jnp.cumsum(n_win, axis=1, dtype=jnp.int32)
    slot_base = slot_end - n_win
    experts = jnp.arange(N_EXPERTS, dtype=jnp.int32)
    per_token = lambda table: jnp.repeat(table.T, TOK_TILE, axis=1)[:, None, :]
    pick = lambda table: jnp.sum(
        jnp.where(idx[None] == experts[:, None, None], per_token(table), 0), axis=0)
    loc = pos - pick(win_start)
    slot = pick(slot_base) + loc // w
    col = jnp.where(slot < slots, slot * w + loc % w, -1)
    q = jnp.arange(slots, dtype=jnp.int32)[None, :, None]
    e_q = jnp.sum((slot_end[:, None, :] <= q).astype(jnp.int32), axis=-1)
    of_slot = lambda table: jnp.sum(jnp.where(e_q[..., None] == experts, table[:, None, :], 0), -1)
    used = q[..., 0] < slot_end[:, -1:]
    slot_start = jnp.where(used, of_slot(win_start) + (q[..., 0] - of_slot(slot_base)) * w, n_rows)
    fb_pass = jnp.where(col < 0, loc // w + 1, 0).reshape(TOP_K, n_tiles, TOK_TILE)
    return col, slot_start, win_start, loc, jnp.max(fb_pass, axis=(0, 2))


def _combine_kernel(tiles_p, n_p, t_p, b_p, t_s, slot_ref, fb_ref, npass_ref, ys_hbm, col_ref,
                    idx_ref, loc_ref, prob_ref, x1p_ref, x1s_ref, mod_ref, gf_ref, yp_ref, ysm_ref,
                    win_buf, sem):
    i = pl.program_id(0)
    n = pl.num_programs(0)
    tm = idx_ref.shape[0]
    slot = i % 2
    w_rows = COMBINE_WINDOW
    w_words = w_rows // 2

    def fetch(starts_ref, count, first, shift, s):
        for j in range(count):
            start = pl.multiple_of(starts_ref[first + j] + shift, F32_SUBLANES)
            pltpu.make_async_copy(ys_hbm.at[pl.ds(start, w_words)],
                                  win_buf.at[s, pl.ds(j * w_words, w_words)],
                                  sem.at[s]).start(priority=j % 2)

    def wait(count, s):
        pltpu.make_async_copy(ys_hbm.at[pl.ds(0, count * w_words)],
                              win_buf.at[s, pl.ds(0, count * w_words)], sem.at[s]).wait()

    def mix(cols, count, s):
        lane = lax.broadcasted_iota(jnp.int32, (tm, count * w_rows), 1)
        prob = prob_ref[...]
        coef = jnp.zeros(lane.shape, F32)
        for k in range(TOP_K):
            coef = jnp.where(lane == cols[:, k:k + 1], prob[:, k:k + 1], coef)
        rows = pltpu.bitcast(win_buf[s, 0:count * w_words, :], BF16)
        return jnp.dot(coef.astype(BF16), rows, preferred_element_type=F32)

    @pl.when(i == 0)
    def _():
        fetch(slot_ref, COMBINE_SLOTS, 0, 0, 0)

    @pl.when(i + 1 < n)
    def _():
        fetch(slot_ref, COMBINE_SLOTS, (i + 1) * COMBINE_SLOTS, 0, 1 - slot)

    wait(COMBINE_SLOTS, slot)
    covered = col_ref[...]
    moe = mix(covered, COMBINE_SLOTS, slot)

    def fallback_pass(p, acc):
        fetch(fb_ref, N_EXPERTS, i * N_EXPERTS, p * w_words, slot)
        wait(N_EXPERTS, slot)
        lw = loc_ref[...] - p * w_rows
        cols = jnp.where((covered < 0) & (lw >= 0) & (lw < w_rows), idx_ref[...] * w_rows + lw, -1)
        return acc + mix(cols, N_EXPERTS, slot)

    moe = lax.fori_loop(0, npass_ref[i], fallback_pass, moe)
    x1 = jnp.where(i < tiles_p, x1p_ref[...], x1s_ref[...])
    gate_rows = []
    for j in range(tm // CHUNK):
        n0 = i * tm + j * CHUNK
        b = jnp.where(n0 < n_p, n0 // t_p, b_p + (n0 - n_p) // t_s)
        gate_rows.append(jnp.broadcast_to(mod_ref[b, 5:6, :], (CHUNK, D_MODEL)))
    x2 = x1 + jnp.concatenate(gate_rows, axis=0) * moe
    y = _rms(x2) * gf_ref[...]

    @pl.when(i < tiles_p)
    def _():
        yp_ref[...] = y

    @pl.when(i >= tiles_p)
    def _():
        ysm_ref[...] = y


def _combine(ys, plan, idx, prob, x1_p, x1_s, mod, t_p, t_s, g_final):
    col, slot_start, fb_start, loc, fb_passes = plan
    col, idx, loc, prob = col.T, idx.T, loc.T, prob.T
    (n_p, d), n_s = x1_p.shape, x1_s.shape[0]
    tm = TOK_TILE
    n = n_p + n_s
    assert t_p % CHUNK == 0 and t_s % CHUNK == 0 and tm % CHUNK == 0
    tiles_p = n_p // tm
    last_p = tiles_p - 1
    spec_p = pl.BlockSpec((tm, d), lambda i, *_: (jnp.minimum(i, last_p), 0))
    spec_s = pl.BlockSpec((tm, d), lambda i, *_: (jnp.maximum(i - tiles_p, 0), 0))
    tok4 = pl.BlockSpec((tm, TOP_K), lambda i, *_: (i, 0))
    return pl.pallas_call(
        functools.partial(_combine_kernel, tiles_p, n_p, t_p, n_p // t_p, t_s),
        grid_spec=pltpu.PrefetchScalarGridSpec(
            num_scalar_prefetch=3,
            grid=(n // tm,),
            in_specs=[pl.BlockSpec(memory_space=pl.ANY), tok4, tok4, tok4, tok4, spec_p, spec_s,
                      pl.BlockSpec(mod.shape, lambda i, *_: (0, 0, 0)),
                      pl.BlockSpec((1, d), lambda i, *_: (0, 0))],
            out_specs=[spec_p, spec_s],
            scratch_shapes=[pltpu.VMEM((2, max(COMBINE_SLOTS, N_EXPERTS) * COMBINE_WINDOW // 2, d),
                                       jnp.uint32),
                            pltpu.SemaphoreType.DMA((2,))]),
        out_shape=[jax.ShapeDtypeStruct((n_p, d), F32), jax.ShapeDtypeStruct((n_s, d), F32)],
        compiler_params=_params("arbitrary"),
        name="combine",
    )((slot_start // 2).reshape(-1), (fb_start // 2).reshape(-1), fb_passes,
      ys, col, idx, loc, prob, x1_p, x1_s, mod, g_final)


def _rope_tables(t, pos0):
    half = RET_QK_DIM // 2
    inv = ROPE_BASE ** (-jnp.arange(0, RET_QK_DIM, 2, dtype=F32) / RET_QK_DIM)
    pos = jnp.arange(t, dtype=F32) + float(pos0)
    ang = pos[:, None] * inv[None, :]
    assert ang.shape == (t, half)
    return jnp.cos(ang), jnp.sin(ang)


def _mixer(x, mod, pos0, conv_cache, s0, wts):
    bsz, t, d = x.shape
    cos, sin = _rope_tables(t, pos0)
    q, k, v, gs, cb, u, sga, sgb = _in_proj(x, mod, wts["g1"], cos, sin, wts["w_in"])
    x1, xn2, s_new, conv_new = _ret(q, k, v, gs, u, cb, sga, sgb, x, mod, s0, conv_cache,
                                    wts["conv_w"], wts["w_conv_out"], wts["w_ret_out"],
                                    wts["w_mix_out"], wts["g2"])
    return x1.reshape(bsz * t, d), xn2.reshape(bsz * t, d), conv_new[None], s_new[None]


def _moe(x1_p, xn2_p, x1_s, xn2_s, mod, t_p, t_s, wts):
    n = x1_p.shape[0] + x1_s.shape[0]
    idx, prob, rank, tile_base, counts = _router(xn2_p, xn2_s, wts["w_router"], wts["b_router"])
    counts = counts.reshape(N_EXPERTS)
    *schedule, off = _visit_schedule(counts, n * TOP_K, EXPERT_ROWS)
    experts = jnp.arange(N_EXPERTS, dtype=jnp.int32)[:, None, None]
    pos = jnp.sum(jnp.where(idx[None] == experts, off[:-1, None, None], 0), axis=0) + rank
    xs = _dispatch(xn2_p, xn2_s, pos)
    ys = _experts(xs, schedule, wts["w_gate"], wts["b_gate"], wts["w_up"], wts["b_up"],
                  wts["w_down"], wts["b_down"])
    plan = _combine_plan(pos, idx, off, counts, tile_base, n * TOP_K)
    return _combine(ys, plan, idx, prob, x1_p, x1_s, mod, t_p, t_s, wts["g_final"])


def kernel(x_prompt, x_sample, c_prompt, c_sample, cache_conv, state_ret, w_ada, b_ada, g_norm1,
           w_in, conv_w, w_ret_out, w_conv_out, w_mix_out, g_norm2, w_router, b_router,
           w_gate, b_gate, w_up, b_up, w_down, b_down, g_final):
    assert w_ada.shape[0] == 1, "single-layer model"
    d = D_MODEL
    bp = x_prompt.shape[0]
    bs = x_sample.shape[0]
    wts = {
        "g1": g_norm1[0].reshape(1, d),
        "w_in": w_in[0].astype(BF16),
        "conv_w": conv_w[0],
        "w_ret_out": w_ret_out[0].astype(BF16),
        "w_conv_out": w_conv_out[0].astype(BF16),
        "w_mix_out": w_mix_out[0].astype(BF16),
        "g2": g_norm2[0].reshape(1, d),
        "w_router": w_router[0].T.astype(BF16),
        "b_router": b_router[0],
        "w_gate": w_gate[0],
        "b_gate": b_gate[0].reshape(N_EXPERTS, 1, d),
        "w_up": w_up[0],
        "b_up": b_up[0].reshape(N_EXPERTS, 1, d),
        "w_down": w_down[0],
        "b_down": b_down[0].reshape(N_EXPERTS, 1, d),
        "g_final": g_final.reshape(1, d),
    }
    mod = _ada(jnp.concatenate([c_prompt, c_sample], axis=0), w_ada[0], b_ada[0])
    mod = mod.reshape(bp + bs, 6, d)
    conv0 = jnp.zeros((bp, CONV_K - 1, d), F32)
    ret0 = jnp.zeros((bp, RET_HEADS, RET_QK_DIM, RET_V_DIM), F32)
    t_p, t_s = x_prompt.shape[1], x_sample.shape[1]
    x1_p, xn2_p, conv_p, ret_p = _mixer(x_prompt, mod[:bp], 0, conv0, ret0, wts)
    x1_s, xn2_s, conv_s, ret_s = _mixer(x_sample, mod[bp:], PAST_LEN, cache_conv[0], state_ret[0],
                                        wts)
    y_p, y_s = _moe(x1_p, xn2_p, x1_s, xn2_s, mod, t_p, t_s, wts)
    return (y_p.reshape(x_prompt.shape), y_s.reshape(x_sample.shape), conv_p, ret_p, conv_s, ret_s)
```
